```python
import math
import jax
import jax.numpy as jnp
from jax import lax
import numpy as np

D_MODEL = 1024
BATCH = 16
SEQ = 2048
DEPTH = 4

CHUNK = 64
CONV_W = 4
D_FF = 4 * D_MODEL
NORM_EPS = 1e-6
L2_EPS = 1e-6
MIX_W = D_MODEL // 2
MIX_OUT = 2 * MIX_W

GDN_HEADS = 4
GDN_DK = MIX_W // GDN_HEADS
GDN_DV = MIX_W // GDN_HEADS
GLA_HEADS = 4
GLA_DK = MIX_W // (2 * GLA_HEADS)
GLA_DV = MIX_W // GLA_HEADS
GLA_GATE_RANK = 16
GLA_TAU = 16.0
GLA_LOG_GATE_MIN = -1.0
SSD_HEADS = 8
SSD_P = MIX_W // SSD_HEADS
SSD_GROUPS = 2
SSD_STATE = 128
RWKV_HEADS = 8
RWKV_DK = MIX_W // RWKV_HEADS
RWKV_W_RANK = 64
RWKV_A_RANK = 64
RWKV_G_RANK = 128
RWKV_GN_EPS = 64e-5

N_EVEN = (DEPTH + 1) // 2
N_ODD = DEPTH // 2

A_QK = GDN_HEADS * GDN_DK
A_V = GDN_HEADS * GDN_DV
A_QKV = 2 * A_QK + A_V
B_QK = GLA_HEADS * GLA_DK
B_V = GLA_HEADS * GLA_DV
EVEN_SPLITS = (A_QKV, A_V, GDN_HEADS, GDN_HEADS, B_QK, B_QK, B_V, B_V, GLA_GATE_RANK)
EVEN_IN = sum(EVEN_SPLITS)

C_X = SSD_HEADS * SSD_P
C_BC = SSD_GROUPS * SSD_STATE
C_XBC = C_X + 2 * C_BC
D_HK = RWKV_HEADS * RWKV_DK
D_SPLITS = (D_HK, D_HK, D_HK, RWKV_W_RANK, RWKV_A_RANK, RWKV_G_RANK)
D_IN = sum(D_SPLITS)
ODD_SPLITS = (C_X, C_XBC, SSD_HEADS, D_IN)
ODD_IN = sum(ODD_SPLITS)

kernel_name = 'hybrid_gdn_gla_ssd_rwkv7_trunk'


def _rmsnorm(x, w):
    xf = x.astype(jnp.float32)
    y = xf * lax.rsqrt(jnp.mean(xf * xf, axis=-1, keepdims=True) + NORM_EPS)
    return (y * w.astype(jnp.float32)).astype(x.dtype)


def _l2norm(x):
    xf = x.astype(jnp.float32)
    return xf * lax.rsqrt(jnp.sum(xf * xf, axis=-1, keepdims=True) + L2_EPS)


def _split(t, sizes):
    parts, off = [], 0
    for s in sizes:
        parts.append(t[..., off:off + s])
        off += s
    return parts


def _causal_dwconv(x, w):
    ch = x.shape[-1]
    return lax.conv_general_dilated(
        x, w[:, None, :].astype(x.dtype), window_strides=(1,),
        padding=[(w.shape[0] - 1, 0)], dimension_numbers=('NWC', 'WIO', 'NWC'),
        feature_group_count=ch)


def _shift(x):
    return jnp.pad(x, ((0, 0), (1, 0), (0, 0)))[:, :-1]


def _to_chunks(t):
    t = t.reshape(t.shape[0], t.shape[1] // CHUNK, CHUNK, *t.shape[2:])
    return jnp.moveaxis(t, 2, 3)


def _from_chunks(t):
    t = jnp.moveaxis(t, 3, 2)
    return t.reshape(t.shape[0], t.shape[1] * t.shape[2], *t.shape[3:])


def _causal_mask(strict=False):
    return jnp.tril(jnp.ones((CHUNK, CHUNK), dtype=bool), -1 if strict else 0)


def _gated_delta_chunked(q, k, v, g, beta):
    bsz, _, nh, dk = q.shape
    dv = v.shape[-1]
    q = _to_chunks(q * dk ** -0.5)
    k = _to_chunks(k)
    v = _to_chunks(v)
    beta = _to_chunks(beta)
    gc = jnp.cumsum(_to_chunks(g), axis=-1)
    dmask = jnp.exp(jnp.where(_causal_mask(), gc[..., :, None] - gc[..., None, :], -jnp.inf))
    kb = k * beta[..., None]
    l_strict = jnp.where(_causal_mask(True), jnp.einsum('bnhik,bnhjk->bnhij', kb, k) * dmask, 0.0)
    eye = jnp.eye(CHUNK, dtype=l_strict.dtype)
    t_inv = lax.linalg.triangular_solve(eye + l_strict, jnp.broadcast_to(eye, l_strict.shape),
                                        left_side=True, lower=True, unit_diagonal=True)
    u = jnp.einsum('bnhij,bnhjv->bnhiv', t_inv, v * beta[..., None])
    w = jnp.einsum('bnhij,bnhjk->bnhik', t_inv, kb * jnp.exp(gc)[..., None])
    a_qk = jnp.einsum('bnhik,bnhjk->bnhij', q, k) * dmask
    qg = q * jnp.exp(gc)[..., None]
    kd = k * jnp.exp(gc[..., -1:] - gc)[..., None]
    g_last = jnp.exp(gc[..., -1])

    def step(state, inp):
        u_n, w_n, qg_n, a_n, kd_n, gl_n = inp
        v_new = u_n - jnp.einsum('bhik,bhkv->bhiv', w_n, state)
        o_n = jnp.einsum('bhik,bhkv->bhiv', qg_n, state) + jnp.einsum('bhij,bhjv->bhiv', a_n, v_new)
        state = state * gl_n[..., None, None] + jnp.einsum('bhik,bhiv->bhkv', kd_n, v_new)
        return state, o_n

    xs = tuple(jnp.moveaxis(t, 1, 0) for t in (u, w, qg, a_qk, kd, g_last))
    _, o = lax.scan(step, jnp.zeros((bsz, nh, dk, dv), q.dtype), xs)
    return _from_chunks(jnp.moveaxis(o, 0, 1))


def _gla_chunked(q, k, v, log_a):
    bsz, _, nh, dk = q.shape
    dv = v.shape[-1]
    gc = jnp.cumsum(_to_chunks(log_a), axis=3)
    q = _to_chunks(q * dk ** -0.5)
    k = _to_chunks(k)
    v = _to_chunks(v)
    qg = q * jnp.exp(gc)
    kg = k * jnp.exp(-gc)
    att = jnp.where(_causal_mask(), jnp.einsum('bnhik,bnhjk->bnhij', qg, kg), 0.0)
    o_intra = jnp.einsum('bnhij,bnhjv->bnhiv', att, v)
    kd = k * jnp.exp(gc[..., -1:, :] - gc)
    g_last = jnp.exp(gc[..., -1, :])

    def step(state, inp):
        kd_n, v_n, gl_n = inp
        new = state * gl_n[..., None] + jnp.einsum('bhik,bhiv->bhkv', kd_n, v_n)
        return new, state

    xs = tuple(jnp.moveaxis(t, 1, 0) for t in (kd, v, g_last))
    _, s_prev = lax.scan(step, jnp.zeros((bsz, nh, dk, dv), q.dtype), xs)
    o_inter = jnp.einsum('bnhik,bnhkv->bnhiv', qg, jnp.moveaxis(s_prev, 0, 1))
    return _from_chunks(o_intra + o_inter)


def _ssd_chunked(x, dt, a_head, b_in, c_in):
    bsz, seq, nh, hp = x.shape
    ng, ns = b_in.shape[2:]
    nr = nh // ng
    nc = seq // CHUNK
    ac = jnp.cumsum((dt * a_head).reshape(bsz, nc, CHUNK, ng, nr), axis=2)
    xdt = (x * dt[..., None]).reshape(bsz, nc, CHUNK, ng, nr, hp)
    bc = b_in.reshape(bsz, nc, CHUNK, ng, ns)
    cc = c_in.reshape(bsz, nc, CHUNK, ng, ns)
    seg = jnp.exp(jnp.where(_causal_mask()[:, :, None, None],
                            ac[:, :, :, None] - ac[:, :, None, :], -jnp.inf))
    cb = jnp.einsum('bcigs,bcjgs->bcijg', cc, bc)
    y_intra = jnp.einsum('bcijgr,bcjgrp->bcigrp', cb[..., None] * seg, xdt)
    st_local = jnp.einsum('bcjgs,bcjgrp->bcgrps', bc, xdt * jnp.exp(ac[:, :, -1:] - ac)[..., None])
    chunk_decay = jnp.exp(ac[:, :, -1])

    def step(state, inp):
        st_n, cd_n = inp
        return state * cd_n[..., None, None] + st_n, state

    _, s_prev = lax.scan(step, jnp.zeros((bsz, ng, nr, hp, ns), x.dtype),
                         (jnp.moveaxis(st_local, 1, 0), jnp.moveaxis(chunk_decay, 1, 0)))
    y_inter = jnp.einsum('bcigs,bcgrps->bcigrp', cc, jnp.moveaxis(s_prev, 0, 1)) * jnp.exp(ac)[..., None]
    return (y_intra + y_inter).reshape(bsz, seq, nh, hp)


def _rwkv7_scan(r, w, k, v, a, b):
    bsz, _, nh, dk = r.shape

    def step(state, inp):
        r_t, w_t, k_t, v_t, a_t, b_t = inp
        sa = jnp.einsum('bhvk,bhk->bhv', state, a_t)
        state = (state * w_t[:, :, None, :] + sa[..., None] * b_t[:, :, None, :]
                 + v_t[..., None] * k_t[:, :, None, :])
        return state, jnp.einsum('bhvk,bhk->bhv', state, r_t)

    xs = tuple(jnp.moveaxis(t, 1, 0) for t in (r, w, k, v, a, b))
    _, ys = lax.scan(step, jnp.zeros((bsz, nh, dk, dk), r.dtype), xs)
    return jnp.moveaxis(ys, 0, 1)


def _even_mixer(h, w_in, conv_w, a_log, dt_bias, gdn_norm_w, gla_w2, gla_b, gla_norm_w, w_out):
    bsz, seq, _ = h.shape
    f32 = jnp.float32
    qkv, z, b_raw, a_raw, gq, gk, gv, gr, g_lr = _split(h @ w_in, EVEN_SPLITS)
    qkv = jax.nn.silu(_causal_dwconv(qkv, conv_w)).astype(f32)
    aq, ak, av = _split(qkv, (A_QK, A_QK, A_V))
    aq = _l2norm(aq.reshape(bsz, seq, GDN_HEADS, GDN_DK))
    ak = _l2norm(ak.reshape(bsz, seq, GDN_HEADS, GDN_DK))
    av = av.reshape(bsz, seq, GDN_HEADS, GDN_DV)
    beta = jax.nn.sigmoid(b_raw.astype(f32))
    g = -jnp.exp(a_log.astype(f32)) * jax.nn.softplus(a_raw.astype(f32) + dt_bias.astype(f32))
    oa = _gated_delta_chunked(aq, ak, av, g, beta)
    oa = _rmsnorm(oa, gdn_norm_w) * jax.nn.silu(z.astype(f32).reshape(bsz, seq, GDN_HEADS, GDN_DV))
    log_a = jnp.maximum(jax.nn.log_sigmoid((g_lr @ gla_w2 + gla_b).astype(f32)) / GLA_TAU,
                        GLA_LOG_GATE_MIN)
    ob = _gla_chunked(gq.astype(f32).reshape(bsz, seq, GLA_HEADS, GLA_DK),
                      gk.astype(f32).reshape(bsz, seq, GLA_HEADS, GLA_DK),
                      gv.astype(f32).reshape(bsz, seq, GLA_HEADS, GLA_DV),
                      log_a.reshape(bsz, seq, GLA_HEADS, GLA_DK))
    ob = _rmsnorm(ob, gla_norm_w) * jax.nn.silu(gr.astype(f32).reshape(bsz, seq, GLA_HEADS, GLA_DV))
    o = jnp.concatenate([oa.reshape(bsz, seq, A_V), ob.reshape(bsz, seq, B_V)], axis=-1)
    return o.astype(h.dtype) @ w_out


def _odd_mixer(h, w_in, conv_w, conv_b, dt_bias, a_log, d_skip, ssd_norm_w, mu, w0, w2, a0, a2, g2,
               k_k, k_a, r_k, gn_w, gn_b, w_out):
    bsz, seq, _ = h.shape
    f32 = jnp.float32
    z, xbc, dt_raw, pd = _split(h @ w_in, ODD_SPLITS)
    xbc = jax.nn.silu(_causal_dwconv(xbc, conv_w) + conv_b).astype(f32)
    xs, b_in, c_in = _split(xbc, (C_X, C_BC, C_BC))
    xs = xs.reshape(bsz, seq, SSD_HEADS, SSD_P)
    dt = jax.nn.softplus(dt_raw.astype(f32) + dt_bias.astype(f32))
    yc = _ssd_chunked(xs, dt, -jnp.exp(a_log.astype(f32)),
                      b_in.reshape(bsz, seq, SSD_GROUPS, SSD_STATE),
                      c_in.reshape(bsz, seq, SSD_GROUPS, SSD_STATE))
    yc = (yc + xs * d_skip.astype(f32)[:, None]).reshape(bsz, seq, C_X) * jax.nn.silu(z.astype(f32))
    yc = _rmsnorm(yc.reshape(bsz, seq, SSD_GROUPS, C_X // SSD_GROUPS),
                  ssd_norm_w.reshape(SSD_GROUPS, C_X // SSD_GROUPS)).reshape(bsz, seq, C_X)
    pd = pd.astype(f32)
    pd = pd + (_shift(pd) - pd) * mu.astype(f32)
    r, k, v, xw, xa, xg = _split(pd, D_SPLITS)
    w_log = -jax.nn.softplus(-(w0 + jnp.tanh(xw) @ w2)) - 0.5
    decay = jnp.exp(-jnp.exp(w_log))
    a = jax.nn.sigmoid(a0 + xa @ a2)
    g = jax.nn.sigmoid(xg) @ g2
    hs = (bsz, seq, RWKV_HEADS, RWKV_DK)
    kk = _l2norm((k * k_k).reshape(hs))
    k = k * (1.0 + (a - 1.0) * k_a)
    r4, k4, v4, a4 = (t.reshape(hs) for t in (r, k, v, a))
    yd = _rwkv7_scan(r4, decay.reshape(hs), k4, v4, -kk, kk * a4)
    mean = jnp.mean(yd, axis=-1, keepdims=True)
    var = jnp.mean(jnp.square(yd - mean), axis=-1, keepdims=True)
    yd = ((yd - mean) * lax.rsqrt(var + RWKV_GN_EPS)).reshape(bsz, seq, D_HK) * gn_w + gn_b
    yd = (yd + (jnp.sum(r4 * k4 * r_k, axis=-1, keepdims=True) * v4).reshape(bsz, seq, D_HK)) * g
    o = jnp.concatenate([yc, yd], axis=-1)
    return o.astype(h.dtype) @ w_out


def _sqrelu_mlp(h, w1, w2):
    return jnp.square(jax.nn.relu(h @ w1)) @ w2


def setup_inputs(seed: int = 0) -> dict:
    key = jax.random.key(seed)
    keys = iter(jax.random.split(key, 48))
    f32 = jnp.float32

    def nrm(shape, scale):
        return jax.random.normal(next(keys), shape, f32) * scale

    def gain(shape):
        return 1.0 + nrm(shape, 0.02)

    def unif(shape, lo, hi):
        return jax.random.uniform(next(keys), shape, f32, lo, hi)

    def dtb(shape):
        dt = jnp.exp(unif(shape, math.log(1e-3), math.log(1e-1)))
        return dt + jnp.log(-jnp.expm1(-dt))

    return {
        'x': nrm((BATCH, SEQ, D_MODEL), 1.0),
        'norm_mix_w': gain((DEPTH, D_MODEL)),
        'norm_mlp_w': gain((DEPTH, D_MODEL)),
        'mlp_w1': nrm((DEPTH, D_MODEL, D_FF), D_MODEL ** -0.5),
        'mlp_w2': nrm((DEPTH, D_FF, D_MODEL), D_FF ** -0.5),
        'final_norm_w': gain((D_MODEL,)),
        'even_w_in': nrm((N_EVEN, D_MODEL, EVEN_IN), D_MODEL ** -0.5),
        'gdn_conv_w': nrm((N_EVEN, CONV_W, A_QKV), CONV_W ** -0.5),
        'gdn_a_log': jnp.log(unif((N_EVEN, GDN_HEADS), 1.0, 16.0)),
        'gdn_dt_bias': dtb((N_EVEN, GDN_HEADS)),
        'gdn_norm_w': gain((N_EVEN, GDN_DV)),
        'gla_gate_w2': nrm((N_EVEN, GLA_GATE_RANK, B_QK), GLA_GATE_RANK ** -0.5),
        'gla_gate_b': nrm((N_EVEN, B_QK), 0.1),
        'gla_norm_w': gain((N_EVEN, GLA_DV)),
        'even_w_out': nrm((N_EVEN, MIX_OUT, D_MODEL), MIX_OUT ** -0.5),
        'odd_w_in': nrm((N_ODD, D_MODEL, ODD_IN), D_MODEL ** -0.5),
        'ssd_conv_w': nrm((N_ODD, CONV_W, C_XBC), CONV_W ** -0.5),
        'ssd_conv_b': nrm((N_ODD, C_XBC), 0.02),
        'ssd_dt_bias': dtb((N_ODD, SSD_HEADS)),
        'ssd_a_log': jnp.log(unif((N_ODD, SSD_HEADS), 1.0, 16.0)),
        'ssd_d': gain((N_ODD, SSD_HEADS)),
        'ssd_norm_w': gain((N_ODD, C_X)),
        'rwkv_mu': unif((N_ODD, D_IN), 0.0, 1.0),
        'rwkv_w0': nrm((N_ODD, D_HK), 0.5),
        'rwkv_w2': nrm((N_ODD, RWKV_W_RANK, D_HK), RWKV_W_RANK ** -0.5),
        'rwkv_a0': nrm((N_ODD, D_HK), 0.1),
        'rwkv_a2': nrm((N_ODD, RWKV_A_RANK, D_HK), RWKV_A_RANK ** -0.5),
        'rwkv_g2': nrm((N_ODD, RWKV_G_RANK, D_HK), RWKV_G_RANK ** -0.5),
        'rwkv_k_k': 0.85 + nrm((N_ODD, D_HK), 0.02),
        'rwkv_k_a': gain((N_ODD, D_HK)),
        'rwkv_r_k': nrm((N_ODD, RWKV_HEADS, RWKV_DK), 0.1),
        'rwkv_gn_w': gain((N_ODD, D_HK)),
        'rwkv_gn_b': nrm((N_ODD, D_HK), 0.02),
        'odd_w_out': nrm((N_ODD, MIX_OUT, D_MODEL), MIX_OUT ** -0.5),
    }


def reference(x, norm_mix_w, norm_mlp_w, mlp_w1, mlp_w2, final_norm_w,
              even_w_in, gdn_conv_w, gdn_a_log, gdn_dt_bias, gdn_norm_w,
              gla_gate_w2, gla_gate_b, gla_norm_w, even_w_out,
              odd_w_in, ssd_conv_w, ssd_conv_b, ssd_dt_bias, ssd_a_log, ssd_d, ssd_norm_w,
              rwkv_mu, rwkv_w0, rwkv_w2, rwkv_a0, rwkv_a2, rwkv_g2, rwkv_k_k, rwkv_k_a,
              rwkv_r_k, rwkv_gn_w, rwkv_gn_b, odd_w_out):
    h = x
    for layer in range(DEPTH):
        i = layer // 2
        hn = _rmsnorm(h, norm_mix_w[layer])
        if layer % 2 == 0:
            mix = _even_mixer(hn, even_w_in[i], gdn_conv_w[i], gdn_a_log[i], gdn_dt_bias[i],
                              gdn_norm_w[i], gla_gate_w2[i], gla_gate_b[i], gla_norm_w[i],
                              even_w_out[i])
        else:
            mix = _odd_mixer(hn, odd_w_in[i], ssd_conv_w[i], ssd_conv_b[i], ssd_dt_bias[i],
                             ssd_a_log[i], ssd_d[i], ssd_norm_w[i], rwkv_mu[i], rwkv_w0[i],
                             rwkv_w2[i], rwkv_a0[i], rwkv_a2[i], rwkv_g2[i], rwkv_k_k[i],
                             rwkv_k_a[i], rwkv_r_k[i], rwkv_gn_w[i], rwkv_gn_b[i], odd_w_out[i])
        h = h + mix.astype(h.dtype)
        h = h + _sqrelu_mlp(_rmsnorm(h, norm_mlp_w[layer]), mlp_w1[layer], mlp_w2[layer]).astype(h.dtype)
    return _rmsnorm(h, final_norm_w)
```

```python
import functools

import jax
import jax.numpy as jnp
from jax import lax
from jax.experimental import pallas as pl
from jax.experimental.pallas import tpu as pltpu

F32 = jnp.float32
BF16 = jnp.bfloat16

D_MODEL = 1024
DEPTH = 4
CHUNK = 64
CONV_W = 4
D_FF = 4 * D_MODEL
NORM_EPS = 1e-6
L2_EPS = 1e-6
MIX_W = D_MODEL // 2
GDN_HEADS = 4
GDN_DK = 128
GLA_HEADS = 4
GLA_DK = 64
GLA_DV = 128
GLA_GATE_RANK = 16
GLA_TAU = 16.0
GLA_LOG_GATE_MIN = -1.0
SSD_HEADS = 8
SSD_P = 64
SSD_GROUPS = 2
SSD_STATE = 128
RWKV_HEADS = 8
RWKV_DK = 64
RWKV_GN_EPS = 64e-5

LANES = 128
CARRY_ROWS = 8
SEQ_BLOCK = 256
MLP_ROWS = 512
MLP_FF_BLOCK = 1024
VMEM_LIMIT = 56 * 1024 * 1024


def _mm(a, b):
    return jnp.dot(a.astype(BF16), b.astype(BF16), preferred_element_type=F32)


def _mm_nt(a, b):
    return lax.dot_general(a.astype(BF16), b.astype(BF16), (((1,), (1,)), ((), ())),
                           preferred_element_type=F32)


def _mm_tn(a, b):
    return lax.dot_general(a.astype(BF16), b.astype(BF16), (((0,), (0,)), ((), ())),
                           preferred_element_type=F32)


def _split(x, n):
    pieces = []
    rem = x
    for i in range(n):
        p = rem.astype(BF16)
        pieces.append(p)
        if i + 1 < n:
            rem = rem - p.astype(F32)
    return pieces


def _dot_x01(x, m01, n=3):
    return sum(jnp.dot(p, m01, preferred_element_type=F32) for p in _split(x, n))


def _dot_01x(m01, x, n=3):
    return sum(jnp.dot(m01, p, preferred_element_type=F32) for p in _split(x, n))


def _rms(x, w, eps=NORM_EPS):
    return x * lax.rsqrt(jnp.mean(x * x, axis=-1, keepdims=True) + eps) * w


def _softplus(x):
    return jnp.maximum(x, 0.0) + jnp.log1p(jnp.exp(-jnp.abs(x)))


def _sigmoid(x):
    return 1.0 / (1.0 + jnp.exp(-x))


def _silu(x):
    return x * _sigmoid(x)


def _iota2(shape, dim):
    return lax.broadcasted_iota(jnp.int32, shape, dim)


def _tri_consts():
    ii = _iota2((CHUNK, CHUNK), 0)
    jj = _iota2((CHUNK, CHUNK), 1)
    causal = ii >= jj
    strict = ii > jj
    ltri = jnp.where(causal, 1.0, 0.0).astype(BF16)
    utri = jnp.where(ii <= jj, 1.0, 0.0).astype(BF16)
    eye = jnp.where(ii == jj, 1.0, 0.0).astype(F32)
    levels = []
    k = 0
    while (1 << k) < CHUNK:
        same = (ii >> (k + 1)) == (jj >> (k + 1))
        lower = ((ii >> k) & 1) == 1
        left = ((jj >> k) & 1) == 0
        levels.append(jnp.where(same & lower & left, 1.0, 0.0).astype(F32))
        k += 1
    return causal, strict, ltri, utri, eye, levels


def _tri_inv(l_strict, eye, levels):
    x = eye - l_strict * levels[0]
    for lvl in levels[1:]:
        c = l_strict * lvl
        x = x - _mm(_mm(x, c), x)
    return x


def _shifted_rows(cur, carry, shift):
    full = jnp.concatenate([carry, cur], axis=0)
    if shift:
        full = pltpu.roll(full, shift, 0)
    return full[CARRY_ROWS:]


def _causal_conv(cur, carry, w):
    out = None
    for j in range(CONV_W):
        term = _shifted_rows(cur, carry, CONV_W - 1 - j) * w[j:j + 1, :]
        out = term if out is None else out + term
    return out


def _mlp_kernel(x_ref, nw_ref, w1_ref, w2_ref, fw_ref, o_ref, *, final_norm):
    x = x_ref[...]
    hn = _rms(x, nw_ref[...]).astype(BF16)
    acc = x
    for c in range(D_FF // MLP_FF_BLOCK):
        cols = slice(c * MLP_FF_BLOCK, (c + 1) * MLP_FF_BLOCK)
        h = jnp.dot(hn, w1_ref[:, cols], preferred_element_type=F32)
        h = jnp.square(jnp.maximum(h, 0.0)).astype(BF16)
        acc = acc + jnp.dot(h, w2_ref[cols, :], preferred_element_type=F32)
    if final_norm:
        acc = _rms(acc, fw_ref[...])
    o_ref[...] = acc


def _mlp_layer(h2d, nw, w1, w2, fw, final_norm):
    t = h2d.shape[0]
    tm = min(MLP_ROWS, t)
    const = lambda i: (0, 0)
    return pl.pallas_call(
        functools.partial(_mlp_kernel, final_norm=final_norm),
        grid=(t // tm,),
        in_specs=[
            pl.BlockSpec((tm, D_MODEL), lambda i: (i, 0)),
            pl.BlockSpec((1, D_MODEL), const),
            pl.BlockSpec((D_MODEL, D_FF), const),
            pl.BlockSpec((D_FF, D_MODEL), const),
            pl.BlockSpec((1, D_MODEL), const),
        ],
        out_specs=pl.BlockSpec((tm, D_MODEL), lambda i: (i, 0)),
        out_shape=jax.ShapeDtypeStruct((t, D_MODEL), F32),
        compiler_params=pltpu.CompilerParams(
            dimension_semantics=("parallel",), vmem_limit_bytes=VMEM_LIMIT),
        name="mlp_final" if final_norm else "mlp",
    )(h2d, nw.reshape(1, D_MODEL), w1.astype(BF16), w2.astype(BF16), fw.reshape(1, D_MODEL))


E_QKV, E_Z, E_GQ, E_GK, E_GV, E_GR, E_B, E_A, E_END = 0, 1536, 2048, 2304, 2560, 3072, 3584, 3712, 3840
E_GLR_LANE = 8


def _even_kernel(x_ref, nw_ref, w_ref, wt_ref, cw_ref, vec_ref, vcol_ref, w2p_ref, wout_ref, o_ref,
                 carry_s, sgdn_s, sgla_s, q_s, k_s, v_s, z_s, gr_s, gq_s, gk_s, gv_s, la_s,
                 g_s, b_s, row_s, oo_s, *, rows):
    nc = rows // CHUNK

    @pl.when(pl.program_id(1) == 0)
    def _():
        carry_s[...] = jnp.zeros_like(carry_s)
        sgdn_s[...] = jnp.zeros_like(sgdn_s)
        sgla_s[...] = jnp.zeros_like(sgla_s)

    x = x_ref[0]
    hn = _rms(x, nw_ref[...]).astype(BF16)
    proj = jnp.dot(hn, w_ref[...], preferred_element_type=F32)
    proj_t = lax.dot_general(wt_ref[...], hn, (((1,), (1,)), ((), ())),
                             preferred_element_type=F32)

    qkv_raw = proj[:, E_QKV:E_Z]
    qkv = _silu(_causal_conv(qkv_raw, carry_s[...], cw_ref[...]))
    carry_s[...] = qkv_raw[rows - CARRY_ROWS:, :]
    for h in range(GDN_HEADS):
        sl = slice(h * GDN_DK, (h + 1) * GDN_DK)
        qh = qkv[:, sl]
        kh = qkv[:, MIX_W + h * GDN_DK:MIX_W + (h + 1) * GDN_DK]
        q_s[:, sl] = qh * (lax.rsqrt(jnp.sum(qh * qh, axis=-1, keepdims=True) + L2_EPS) * GDN_DK ** -0.5)
        k_s[:, sl] = kh * lax.rsqrt(jnp.sum(kh * kh, axis=-1, keepdims=True) + L2_EPS)
    v_s[...] = qkv[:, 2 * MIX_W:]
    z_s[...] = proj[:, E_Z:E_GQ]

    alog = vec_ref[0:1, 0:LANES]
    dtb = vec_ref[1:2, 0:LANES]
    g_s[...] = -jnp.exp(alog) * _softplus(proj[:, E_A:E_END] + dtb)
    pb = proj[:, E_B:E_A]
    b_s[...] = _sigmoid(pb)
    row_id = _iota2(proj_t.shape, 0)
    g_row = -jnp.exp(vcol_ref[:, 0:1]) * _softplus(proj_t + vcol_ref[:, 1:2])
    gb_row = jnp.where(row_id < GDN_HEADS, g_row, _sigmoid(proj_t))
    for c in range(nc):
        row_s[c] = gb_row[:, c * CHUNK:(c + 1) * CHUNK]

    gq_s[...] = proj[:, E_GQ:E_GK] * GLA_DK ** -0.5
    gk_s[...] = proj[:, E_GK:E_GV]
    gv_s[...] = proj[:, E_GV:E_GR]
    gr_s[...] = proj[:, E_GR:E_B]
    gate = _mm(pb, w2p_ref[...]) + vec_ref[3:4, :]
    la_s[...] = jnp.maximum(-_softplus(-gate) / GLA_TAU, GLA_LOG_GATE_MIN)

    causal, strict, ltri, utri, eye, levels = _tri_consts()
    lane = _iota2((1, LANES), 1)

    def chunk_body(c, carry):
        r0 = pl.multiple_of(c * CHUNK, CHUNK)
        rs = pl.ds(r0, CHUNK)
        gc128 = _dot_01x(ltri, g_s[rs, :])
        b128 = b_s[rs, :]
        gbr = row_s[c]
        gcrow = _dot_x01(gbr, utri)
        for h in range(GDN_HEADS):
            sl = slice(h * GDN_DK, (h + 1) * GDN_DK)
            qh, kh, vh = q_s[rs, sl], k_s[rs, sl], v_s[rs, sl]
            bcol = b128[:, h:h + 1]
            gcc = gc128[:, h:h + 1]
            gcr = gcrow[h:h + 1, :]
            dmask = jnp.where(causal, jnp.exp(jnp.where(causal, gcc - gcr, 0.0)), 0.0)
            kb = kh * bcol
            l_strict = jnp.where(strict, _mm_nt(kb, kh) * dmask, 0.0)
            t_inv = _tri_inv(l_strict, eye, levels)
            egc = jnp.exp(gcc)
            u = _mm(t_inv, vh * bcol)
            w = _mm(t_inv, kb * egc)
            a_qk = _mm_nt(qh, kh) * dmask
            gcl = gcc[CHUNK - 1:CHUNK, :]
            state = sgdn_s[h]
            v_new = u - _mm(w, state)
            oo_s[rs, sl] = _mm(qh * egc, state) + _mm(a_qk, v_new)
            sgdn_s[h] = state * jnp.exp(gcl) + _mm_tn(kh * jnp.exp(gcl - gcc), v_new)
        for p in range(GLA_HEADS // 2):
            psl = slice(p * LANES, (p + 1) * LANES)
            gc = _dot_01x(ltri, la_s[rs, psl])
            gcl = gc[CHUNK - 1:CHUNK, :]
            qg = gq_s[rs, psl] * jnp.exp(gc)
            kraw = gk_s[rs, psl]
            kg = kraw * jnp.exp(-gc)
            kd = kraw * jnp.exp(gcl - gc)
            st = sgla_s[p]
            upd = st * jnp.exp(gcl)
            for e in range(2):
                h = 2 * p + e
                m = (lane >= e * GLA_DK) & (lane < (e + 1) * GLA_DK)
                vsl = slice(h * GLA_DV, (h + 1) * GLA_DV)
                qgm = jnp.where(m, qg, 0.0)
                att = jnp.where(causal, _mm_nt(qgm, kg), 0.0)
                vh = gv_s[rs, vsl]
                oo_s[rs, MIX_W + h * GLA_DV:MIX_W + (h + 1) * GLA_DV] = _mm(att, vh) + _mm_nt(qgm, st)
                upd = upd + _mm_tn(vh, jnp.where(m, kd, 0.0))
            sgla_s[p] = upd
        return carry

    lax.fori_loop(0, nc, chunk_body, 0)

    gdn_w = vec_ref[2:3, 0:LANES]
    gla_w = vec_ref[4:5, 0:LANES]
    for h in range(GDN_HEADS + GLA_HEADS):
        sl = slice(h * LANES, (h + 1) * LANES)
        if h < GDN_HEADS:
            gate, nw = z_s[:, sl], gdn_w
        else:
            gate, nw = gr_s[:, slice((h - GDN_HEADS) * LANES, (h - GDN_HEADS + 1) * LANES)], gla_w
        oo_s[:, sl] = _rms(oo_s[:, sl], nw) * _silu(gate)
    o_ref[0] = x + jnp.dot(oo_s[...].astype(BF16), wout_ref[...], preferred_element_type=F32)


def _pad_cols(w, width, at=0):
    out = jnp.zeros((w.shape[0], width), w.dtype)
    return out.at[:, at:at + w.shape[1]].set(w)


def _even_layer(h, nw, w_in, conv_w, a_log, dt_bias, gdn_norm_w, gla_w2, gla_b, gla_norm_w, w_out):
    bsz, seq, _ = h.shape
    rows = min(SEQ_BLOCK, seq)
    nc = rows // CHUNK
    o = 0
    parts = []
    for s in (1536, 512, 4, 4, 256, 256, 512, 512, 16):
        parts.append(w_in[:, o:o + s])
        o += s
    w_qkv, w_z, w_b, w_a, w_gq, w_gk, w_gv, w_gr, w_glr = parts
    w_b128 = _pad_cols(w_b, LANES).at[:, E_GLR_LANE:E_GLR_LANE + GLA_GATE_RANK].set(w_glr)
    w_pack = jnp.concatenate([w_qkv, w_z, w_gq, w_gk, w_gv, w_gr, w_b128, _pad_cols(w_a, LANES)],
                             axis=1).astype(BF16)
    w_t = jnp.concatenate([w_a.T, w_b.T], axis=0).astype(BF16)
    vec = jnp.zeros((8, 2 * LANES), F32)
    vec = vec.at[0, :GDN_HEADS].set(a_log).at[1, :GDN_HEADS].set(dt_bias)
    vec = vec.at[2, :LANES].set(gdn_norm_w).at[3, :].set(gla_b).at[4, :LANES].set(gla_norm_w)
    vcol = jnp.zeros((8, LANES), F32).at[:GDN_HEADS, 0].set(a_log).at[:GDN_HEADS, 1].set(dt_bias)
    w2p = jnp.zeros((LANES, GLA_HEADS * GLA_DK), F32).at[E_GLR_LANE:E_GLR_LANE + GLA_GATE_RANK].set(gla_w2)

    def full(a):
        return pl.BlockSpec(a.shape, lambda b, j: (0,) * a.ndim)

    consts = (nw.reshape(1, D_MODEL), w_pack, w_t, conv_w, vec, vcol, w2p.astype(BF16), w_out.astype(BF16))
    blk = pl.BlockSpec((1, rows, D_MODEL), lambda b, j: (b, j, 0))
    vm = lambda *s: pltpu.VMEM(s, F32)
    return pl.pallas_call(
        functools.partial(_even_kernel, rows=rows),
        grid=(bsz, seq // rows),
        in_specs=[blk] + [full(a) for a in consts],
        out_specs=blk,
        out_shape=jax.ShapeDtypeStruct(h.shape, F32),
        scratch_shapes=[
            vm(CARRY_ROWS, 3 * MIX_W), vm(GDN_HEADS, GDN_DK, GDN_DK), vm(GLA_HEADS // 2, GLA_DV, LANES),
            vm(rows, MIX_W), vm(rows, MIX_W), vm(rows, MIX_W), vm(rows, MIX_W), vm(rows, MIX_W),
            vm(rows, MIX_W // 2), vm(rows, MIX_W // 2), vm(rows, MIX_W), vm(rows, MIX_W // 2),
            vm(rows, LANES), vm(rows, LANES), vm(nc, 8, CHUNK), vm(rows, D_MODEL),
        ],
        compiler_params=pltpu.CompilerParams(
            dimension_semantics=("parallel", "arbitrary"), vmem_limit_bytes=VMEM_LIMIT),
        name="even_mixer",
    )(h, *consts)


O_Z, O_XBC, O_PD, O_DT, O_END = 0, 512, 1536, 3328, 3456
PD_W = 1792
C_X = SSD_HEADS * SSD_P
C_BC = SSD_GROUPS * SSD_STATE
D_HK = RWKV_HEADS * RWKV_DK


def _odd_kernel(x_ref, nw_ref, w_ref, wt_ref, cw_ref, cb_ref, mu_ref, vec_ref, vcol_ref,
                w2p_ref, a2p_ref, g2_ref, exp_ref, bones_ref, wout_ref, o_ref,
                carry_s, pdc_s, sssd_s, srw_s,
                z_s, xs_s, xdt_s, bi_s, ci_s, da_s, darow_s,
                r_s, k_s, v_s, al_s, be_s, lw_s, g_s, yy_s, *, rows):
    nc = rows // CHUNK

    @pl.when(pl.program_id(1) == 0)
    def _():
        carry_s[...] = jnp.zeros_like(carry_s)
        pdc_s[...] = jnp.zeros_like(pdc_s)
        sssd_s[...] = jnp.zeros_like(sssd_s)
        srw_s[...] = jnp.zeros_like(srw_s)

    x = x_ref[0]
    hn = _rms(x, nw_ref[...]).astype(BF16)
    proj = jnp.dot(hn, w_ref[...], preferred_element_type=F32)
    proj_t = lax.dot_general(wt_ref[...], hn, (((1,), (1,)), ((), ())),
                             preferred_element_type=F32)

    expand = exp_ref[...]
    bones = bones_ref[...]

    z_s[...] = proj[:, O_Z:O_XBC]
    xbc_raw = proj[:, O_XBC:O_PD]
    xbc = _silu(_causal_conv(xbc_raw, carry_s[...], cw_ref[...]) + cb_ref[...])
    carry_s[...] = xbc_raw[rows - CARRY_ROWS:, :]
    xs = xbc[:, :C_X]
    xs_s[...] = xs
    bi_s[...] = xbc[:, C_X:C_X + C_BC]
    ci_s[...] = xbc[:, C_X + C_BC:]
    dtb = vec_ref[0:1, 0:LANES]
    alog = vec_ref[1:2, 0:LANES]
    dt128 = _softplus(proj[:, O_DT:O_END] + dtb)
    da_s[...] = dt128 * -jnp.exp(alog)
    xdt_s[...] = xs * _dot_x01(dt128, expand)
    da_row = _softplus(proj_t + vcol_ref[:, 0:1]) * -jnp.exp(vcol_ref[:, 1:2])
    for c in range(nc):
        darow_s[c] = da_row[:, c * CHUNK:(c + 1) * CHUNK]

    pd = proj[:, O_PD:O_DT]
    pd = pd + (_shifted_rows(pd, pdc_s[...], 1) - pd) * mu_ref[...]
    pdc_s[...] = proj[rows - CARRY_ROWS:, O_PD:O_DT]
    r = pd[:, 0:D_HK]
    k = pd[:, D_HK:2 * D_HK]
    v_s[...] = pd[:, 2 * D_HK:3 * D_HK]
    xwa = pd[:, 3 * D_HK:3 * D_HK + LANES]
    xg = pd[:, 3 * D_HK + LANES:]
    w0, a0 = vec_ref[2:3, :], vec_ref[3:4, :]
    k_k, k_a, r_k = vec_ref[4:5, :], vec_ref[5:6, :], vec_ref[6:7, :]
    w_log = -_softplus(-(w0 + _mm(jnp.tanh(xwa), w2p_ref[...]))) - 0.5
    lw_s[...] = -jnp.exp(w_log)
    a = _sigmoid(a0 + _mm(xwa, a2p_ref[...]))
    g_s[...] = _mm(_sigmoid(xg), g2_ref[...])
    kkr = k * k_k
    kk = kkr * lax.rsqrt(_dot_x01(kkr * kkr, bones, 2) + L2_EPS)
    k2 = k * (1.0 + (a - 1.0) * k_a)
    r_s[...] = r
    k_s[...] = k2
    al_s[...] = -kk
    be_s[...] = kk * a

    causal, strict, ltri, utri, eye, levels = _tri_consts()
    lane = _iota2((1, LANES), 1)
    lane_g = _iota2((1, SSD_GROUPS * LANES), 1)
    row_p = _iota2((LANES, LANES), 0)
    col_p = _iota2((LANES, LANES), 1)
    blockdiag = (row_p >= RWKV_DK) == (col_p >= RWKV_DK)

    def chunk_body(c, carry):
        r0 = pl.multiple_of(c * CHUNK, CHUNK)
        rs = pl.ds(r0, CHUNK)

        ac128 = _dot_01x(ltri, da_s[rs, :])
        acrow = _dot_x01(darow_s[c], utri)
        ac_b = _dot_x01(ac128, expand)
        acl_b = ac_b[CHUNK - 1:CHUNK, :]
        e_ac = jnp.exp(ac_b)
        e_dec = jnp.exp(acl_b - ac_b)
        cd = jnp.exp(acl_b)
        gw = SSD_GROUPS * LANES
        for g in range(SSD_GROUPS):
            gsl = slice(g * gw, (g + 1) * gw)
            ssl = slice(g * SSD_STATE, (g + 1) * SSD_STATE)
            cg, bg = ci_s[rs, ssl], bi_s[rs, ssl]
            cbm = _mm_nt(cg, bg)
            state = sssd_s[g]
            xg_ = xdt_s[rs, gsl]
            y = _mm(cg, state) * e_ac[:, gsl]
            for rr in range(SSD_HEADS // SSD_GROUPS):
                hh = g * (SSD_HEADS // SSD_GROUPS) + rr
                seg = jnp.where(causal, jnp.exp(jnp.where(
                    causal, ac128[:, hh:hh + 1] - acrow[hh:hh + 1, :], 0.0)), 0.0)
                m = (lane_g >= rr * SSD_P) & (lane_g < (rr + 1) * SSD_P)
                y = y + _mm(cbm * seg, jnp.where(m, xg_, 0.0))
            yy_s[rs, gsl] = y
            sssd_s[g] = state * cd[:, gsl] + _mm_tn(bg, xg_ * e_dec[:, gsl])

        lw = lw_s[rs, :]
        p = _dot_01x(ltri, lw)
        p_last = p[CHUNK - 1:CHUNK, :]
        e_np = jnp.exp(-p)
        e_dec_r = jnp.exp(p_last - p)
        al, be, k2c, rc = al_s[rs, :], be_s[rs, :], k_s[rs, :], r_s[rs, :]
        a_hat = al * jnp.exp(p - lw)
        b_hat = be * e_np
        k_hat = k2c * e_np
        r_hat = rc * jnp.exp(p)
        b_til = be * e_dec_r
        k_til = k2c * e_dec_r
        p_c = jnp.exp(p_last)
        for pr in range(RWKV_HEADS // 2):
            psl = slice(pr * LANES, (pr + 1) * LANES)
            state = srw_s[pr]
            vp = v_s[rs, psl]
            bp, kp = b_hat[:, psl], k_hat[:, psl]
            us, ys = [], []
            for e in range(2):
                m = (lane >= e * RWKV_DK) & (lane < (e + 1) * RWKV_DK)
                am = jnp.where(m, a_hat[:, psl], 0.0)
                rm = jnp.where(m, r_hat[:, psl], 0.0)
                m_ab = jnp.where(strict, _mm_nt(am, bp), 0.0)
                m_ak = jnp.where(strict, _mm_nt(am, kp), 0.0)
                m_rb = jnp.where(causal, _mm_nt(rm, bp), 0.0)
                m_rk = jnp.where(causal, _mm_nt(rm, kp), 0.0)
                t_inv = _tri_inv(-m_ab, eye, levels)
                u_e = _mm(t_inv, _mm_nt(am, state) + _mm(m_ak, vp))
                ys.append(_mm_nt(rm, state) + _mm(m_rb, u_e) + _mm(m_rk, vp))
                us.append(u_e)
            first = lane < RWKV_DK
            u = jnp.where(first, us[0], us[1])
            yy_s[rs, C_X + pr * LANES:C_X + (pr + 1) * LANES] = jnp.where(first, ys[0], ys[1])
            new = state * p_c[:, psl] + _mm_tn(u, b_til[:, psl]) + _mm_tn(vp, k_til[:, psl])
            srw_s[pr] = jnp.where(blockdiag, new, 0.0)
        return carry

    lax.fori_loop(0, nc, chunk_body, 0)

    d_skip = vec_ref[7:8, :]
    yc = (yy_s[:, 0:C_X] + xs_s[...] * d_skip) * _silu(z_s[...])
    gw = C_X // SSD_GROUPS
    for g in range(SSD_GROUPS):
        gsl = slice(g * gw, (g + 1) * gw)
        yy_s[:, gsl] = _rms(yc[:, gsl], vec_ref[8:9, gsl])
    yd = yy_s[:, C_X:]
    inv_n = 1.0 / RWKV_DK
    mean = _dot_x01(yd, bones, 2) * inv_n
    cen = yd - mean
    var = _dot_x01(cen * cen, bones, 2) * inv_n
    yd = cen * lax.rsqrt(var + RWKV_GN_EPS) * vec_ref[9:10, :] + vec_ref[10:11, :]
    bonus = _dot_x01(r_s[...] * k_s[...] * r_k, bones, 2)
    yy_s[:, C_X:] = (yd + bonus * v_s[...]) * g_s[...]
    o_ref[0] = x + jnp.dot(yy_s[...].astype(BF16), wout_ref[...], preferred_element_type=F32)


def _odd_layer(h, nw, w_in, conv_w, conv_b, dt_bias, a_log, d_skip, ssd_norm_w, mu, w0, w2, a0, a2, g2,
               k_k, k_a, r_k, gn_w, gn_b, w_out):
    bsz, seq, _ = h.shape
    rows = min(SEQ_BLOCK, seq)
    nc = rows // CHUNK
    w_z = w_in[:, 0:512]
    w_xbc = w_in[:, 512:1536]
    w_dt = w_in[:, 1536:1544]
    w_pd = w_in[:, 1544:1544 + PD_W]
    w_pack = jnp.concatenate([w_z, w_xbc, w_pd, _pad_cols(w_dt, LANES)], axis=1).astype(BF16)
    w_t = w_dt.T.astype(BF16)
    vec = jnp.zeros((16, D_HK), F32)
    vec = vec.at[0, :SSD_HEADS].set(dt_bias).at[1, :SSD_HEADS].set(a_log)
    vec = vec.at[2].set(w0).at[3].set(a0).at[4].set(k_k).at[5].set(k_a).at[6].set(r_k.reshape(-1))
    vec = vec.at[7].set(jnp.repeat(d_skip, SSD_P)).at[8].set(ssd_norm_w).at[9].set(gn_w).at[10].set(gn_b)
    vcol = jnp.zeros((8, LANES), F32).at[:, 0].set(dt_bias).at[:, 1].set(a_log)
    rank = w2.shape[0]
    w2p = jnp.zeros((LANES, D_HK), F32).at[:rank].set(w2).astype(BF16)
    a2p = jnp.zeros((LANES, D_HK), F32).at[rank:rank + a2.shape[0]].set(a2).astype(BF16)
    head_of_lane = jnp.arange(D_HK) // RWKV_DK
    expand = (jnp.arange(LANES)[:, None] == head_of_lane[None, :]).astype(BF16)
    bones = (head_of_lane[:, None] == head_of_lane[None, :]).astype(BF16)

    def full(a):
        return pl.BlockSpec(a.shape, lambda b, j: (0,) * a.ndim)

    consts = (nw.reshape(1, D_MODEL), w_pack, w_t, conv_w, conv_b.reshape(1, -1), mu[:PD_W].reshape(1, PD_W),
              vec, vcol, w2p, a2p, g2.astype(BF16), expand, bones, w_out.astype(BF16))
    blk = pl.BlockSpec((1, rows, D_MODEL), lambda b, j: (b, j, 0))
    vm = lambda *s: pltpu.VMEM(s, F32)
    return pl.pallas_call(
        functools.partial(_odd_kernel, rows=rows),
        grid=(bsz, seq // rows),
        in_specs=[blk] + [full(a) for a in consts],
        out_specs=blk,
        out_shape=jax.ShapeDtypeStruct(h.shape, F32),
        scratch_shapes=[
            vm(CARRY_ROWS, C_X + 2 * C_BC), vm(CARRY_ROWS, PD_W),
            vm(SSD_GROUPS, SSD_STATE, SSD_GROUPS * LANES), vm(RWKV_HEADS // 2, LANES, LANES),
            vm(rows, C_X), vm(rows, C_X), vm(rows, C_X), vm(rows, C_BC), vm(rows, C_BC),
            vm(rows, LANES), vm(nc, 8, CHUNK),
            vm(rows, D_HK), vm(rows, D_HK), vm(rows, D_HK), vm(rows, D_HK), vm(rows, D_HK),
            vm(rows, D_HK), vm(rows, D_HK), vm(rows, D_MODEL),
        ],
        compiler_params=pltpu.CompilerParams(
            dimension_semantics=("parallel", "arbitrary"), vmem_limit_bytes=VMEM_LIMIT),
        name="odd_mixer",
    )(h, *consts)


def kernel(x, norm_mix_w, norm_mlp_w, mlp_w1, mlp_w2, final_norm_w, even_w_in, gdn_conv_w, gdn_a_log, gdn_dt_bias, gdn_norm_w, gla_gate_w2, gla_gate_b, gla_norm_w, even_w_out, odd_w_in, ssd_conv_w, ssd_conv_b, ssd_dt_bias, ssd_a_log, ssd_d, ssd_norm_w, rwkv_mu, rwkv_w0, rwkv_w2, rwkv_a0, rwkv_a2, rwkv_g2, rwkv_k_k, rwkv_k_a, rwkv_r_k, rwkv_gn_w, rwkv_gn_b, odd_w_out):
    bsz, seq, _ = x.shape
    h = x
    for layer in range(DEPTH):
        i = layer // 2
        if layer % 2 == 0:
            h = _even_layer(h, norm_mix_w[layer], even_w_in[i], gdn_conv_w[i], gdn_a_log[i], gdn_dt_bias[i],
                            gdn_norm_w[i], gla_gate_w2[i], gla_gate_b[i], gla_norm_w[i], even_w_out[i])
        else:
            h = _odd_layer(h, norm_mix_w[layer], odd_w_in[i], ssd_conv_w[i], ssd_conv_b[i], ssd_dt_bias[i],
                           ssd_a_log[i], ssd_d[i], ssd_norm_w[i], rwkv_mu[i], rwkv_w0[i], rwkv_w2[i],
                           rwkv_a0[i], rwkv_a2[i], rwkv_g2[i], rwkv_k_k[i], rwkv_k_a[i], rwkv_r_k[i],
                           rwkv_gn_w[i], rwkv_gn_b[i], odd_w_out[i])
        last = layer == DEPTH - 1
        h = _mlp_layer(h.reshape(bsz * seq, D_MODEL), norm_mlp_w[layer], mlp_w1[layer], mlp_w2[layer],
                       final_norm_w, last).reshape(bsz, seq, D_MODEL)
    return h
```

```python
import functools

import jax
import jax.numpy as jnp
from jax import lax
from jax.experimental import pallas as pl
from jax.experimental.pallas import tpu as pltpu

F32 = jnp.float32
BF16 = jnp.bfloat16

D_MODEL = 1024
DEPTH = 4
CHUNK = 64
CONV_W = 4
D_FF = 4 * D_MODEL
NORM_EPS = 1e-6
L2_EPS = 1e-6
MIX_W = D_MODEL // 2
GDN_HEADS = 4
GDN_DK = 128
GLA_HEADS = 4
GLA_DK = 64
GLA_DV = 128
GLA_GATE_RANK = 16
GLA_TAU = 16.0
GLA_LOG_GATE_MIN = -1.0
SSD_HEADS = 8
SSD_P = 64
SSD_GROUPS = 2
SSD_STATE = 128
RWKV_HEADS = 8
RWKV_DK = 64
RWKV_GN_EPS = 64e-5

LANES = 128
CARRY_ROWS = 8
GROUP = 4
GROUP_W = GROUP * CHUNK
SEQ_BLOCK = 256
MLP_ROWS = 512
MLP_FF_BLOCK = 1024
VMEM_LIMIT = 56 * 1024 * 1024


def _mm(a, b):
    return jnp.dot(a.astype(BF16), b.astype(BF16), preferred_element_type=F32)


def _mm_nt(a, b):
    return lax.dot_general(a.astype(BF16), b.astype(BF16), (((1,), (1,)), ((), ())),
                           preferred_element_type=F32)


def _mm_tn(a, b):
    return lax.dot_general(a.astype(BF16), b.astype(BF16), (((0,), (0,)), ((), ())),
                           preferred_element_type=F32)


def _split(x, n):
    pieces = []
    rem = x
    for i in range(n):
        p = rem.astype(BF16)
        pieces.append(p)
        if i + 1 < n:
            rem = rem - p.astype(F32)
    return pieces


def _dot_x01(x, m01, n=3):
    return sum(jnp.dot(p, m01, preferred_element_type=F32) for p in _split(x, n))


def _dot_01x(m01, x, n=3):
    return sum(jnp.dot(m01, p, preferred_element_type=F32) for p in _split(x, n))


def _rms(x, w, eps=NORM_EPS):
    return x * lax.rsqrt(jnp.mean(x * x, axis=-1, keepdims=True) + eps) * w


def _softplus(x):
    return jnp.maximum(x, 0.0) + jnp.log1p(jnp.exp(-jnp.abs(x)))


def _sigmoid(x):
    return 1.0 / (1.0 + jnp.exp(-x))


def _silu(x):
    return x * _sigmoid(x)


def _iota2(shape, dim):
    return lax.broadcasted_iota(jnp.int32, shape, dim)


def _log2(n):
    assert n & (n - 1) == 0
    return n.bit_length() - 1


def _lane_blocks(nblk, width):
    lane = _iota2((1, nblk * width), 1)
    return [(lane >= h * width) & (lane < (h + 1) * width) for h in range(nblk)]


def _bd(x, masks):
    return jnp.concatenate([jnp.where(m, x, 0.0).astype(BF16) for m in masks], axis=0)


def _group_consts():
    ii = _iota2((CHUNK, GROUP_W), 0)
    jj = _iota2((CHUNK, GROUP_W), 1) & (CHUNK - 1)
    causal = ii >= jj
    strict = ii > jj
    eye = jnp.where(ii == jj, 1.0, 0.0).astype(F32)
    levels = []
    k = 0
    while (1 << k) < CHUNK:
        same = (ii >> (k + 1)) == (jj >> (k + 1))
        lower = ((ii >> k) & 1) == 1
        left = ((jj >> k) & 1) == 0
        levels.append(jnp.where(same & lower & left, 1.0, 0.0).astype(F32))
        k += 1
    ones = jnp.ones((CHUNK, CHUNK), BF16)
    return causal, strict, eye, levels, ones


def _chunk_tril(rows):
    ii = _iota2((rows, rows), 0)
    jj = _iota2((rows, rows), 1)
    return jnp.where(((ii >> _log2(CHUNK)) == (jj >> _log2(CHUNK))) & (ii >= jj), 1.0, 0.0).astype(BF16)


def _tri_inv_many(ls, eye, levels, blocks):
    xs = [eye - l * levels[0] for l in ls]
    for lvl in levels[1:]:
        ys = [_mm(x, _bd(l * lvl, blocks)) for x, l in zip(xs, ls)]
        xs = [x - _mm(y, _bd(x, blocks)) for x, y in zip(xs, ys)]
    return xs


def _rowcast(colx, eye, ones):
    return _dot_01x(ones, colx * eye)


def _decay_mask(colx, mask, eye, ones):
    return jnp.where(mask, jnp.exp(jnp.where(mask, colx - _rowcast(colx, eye, ones), 0.0)), 0.0)


def _shifted_rows(cur, carry, shift):
    full = jnp.concatenate([carry, cur], axis=0)
    if shift:
        full = pltpu.roll(full, shift, 0)
    return full[CARRY_ROWS:]


def _causal_conv(cur, carry, w):
    out = None
    for j in range(CONV_W):
        term = _shifted_rows(cur, carry, CONV_W - 1 - j) * w[j:j + 1, :]
        out = term if out is None else out + term
    return out


def _mlp_kernel(x_ref, nw_ref, w1_ref, w2_ref, fw_ref, o_ref, *, final_norm):
    x = x_ref[...]
    hn = _rms(x, nw_ref[...]).astype(BF16)
    acc = x
    for c in range(D_FF // MLP_FF_BLOCK):
        cols = slice(c * MLP_FF_BLOCK, (c + 1) * MLP_FF_BLOCK)
        h = jnp.dot(hn, w1_ref[:, cols], preferred_element_type=F32)
        h = jnp.square(jnp.maximum(h, 0.0)).astype(BF16)
        acc = acc + jnp.dot(h, w2_ref[cols, :], preferred_element_type=F32)
    if final_norm:
        acc = _rms(acc, fw_ref[...])
    o_ref[...] = acc


def _mlp_layer(h2d, nw, w1, w2, fw, final_norm):
    t = h2d.shape[0]
    tm = min(MLP_ROWS, t)
    const = lambda i: (0, 0)
    return pl.pallas_call(
        functools.partial(_mlp_kernel, final_norm=final_norm),
        grid=(t // tm,),
        in_specs=[
            pl.BlockSpec((tm, D_MODEL), lambda i: (i, 0)),
            pl.BlockSpec((1, D_MODEL), const),
            pl.BlockSpec((D_MODEL, D_FF), const),
            pl.BlockSpec((D_FF, D_MODEL), const),
            pl.BlockSpec((1, D_MODEL), const),
        ],
        out_specs=pl.BlockSpec((tm, D_MODEL), lambda i: (i, 0)),
        out_shape=jax.ShapeDtypeStruct((t, D_MODEL), F32),
        compiler_params=pltpu.CompilerParams(
            dimension_semantics=("parallel",), vmem_limit_bytes=VMEM_LIMIT),
        name="mlp_final" if final_norm else "mlp",
    )(h2d, nw.reshape(1, D_MODEL), w1.astype(BF16), w2.astype(BF16), fw.reshape(1, D_MODEL))


E_QKV, E_Z, E_GQ, E_GK, E_GV, E_GR, E_B, E_A, E_END = 0, 1536, 2048, 2304, 2560, 3072, 3584, 3712, 3840
E_GLR_LANE = 8


def _even_kernel(x_ref, nw_ref, w_ref, cw_ref, vec_ref, w2p_ref, e64_ref, e128_ref, wout_ref, o_ref,
                 carry_s, sgdn_s, sgla_s, oo_s, *, rows):
    nc = rows // CHUNK

    @pl.when(pl.program_id(1) == 0)
    def _():
        carry_s[...] = jnp.zeros_like(carry_s)
        sgdn_s[...] = jnp.zeros_like(sgdn_s)
        sgla_s[...] = jnp.zeros_like(sgla_s)

    x = x_ref[0]
    hn = _rms(x, nw_ref[...]).astype(BF16)
    proj = jnp.dot(hn, w_ref[...], preferred_element_type=F32)

    causal, strict, eye, levels, ones = _group_consts()
    tril = _chunk_tril(rows)
    blk64 = _lane_blocks(GROUP, CHUNK)
    blk128 = _lane_blocks(GDN_HEADS, GDN_DK)
    e64, e128 = e64_ref[...], e128_ref[...]

    qkv_raw = proj[:, E_QKV:E_Z]
    qkv = _silu(_causal_conv(qkv_raw, carry_s[...], cw_ref[...]))
    carry_s[...] = qkv_raw[rows - CARRY_ROWS:, :]
    qs, ks = [], []
    for h in range(GDN_HEADS):
        qh = qkv[:, h * GDN_DK:(h + 1) * GDN_DK]
        kh = qkv[:, MIX_W + h * GDN_DK:MIX_W + (h + 1) * GDN_DK]
        qs.append(qh * (lax.rsqrt(jnp.sum(qh * qh, axis=-1, keepdims=True) + L2_EPS) * GDN_DK ** -0.5))
        ks.append(kh * lax.rsqrt(jnp.sum(kh * kh, axis=-1, keepdims=True) + L2_EPS))
    q4 = jnp.concatenate(qs, axis=1)
    k4 = jnp.concatenate(ks, axis=1)
    v4 = qkv[:, 2 * MIX_W:]

    alog = vec_ref[0:1, 0:LANES]
    dtb = vec_ref[1:2, 0:LANES]
    g128 = -jnp.exp(alog) * _softplus(proj[:, E_A:E_END] + dtb)
    pb = proj[:, E_B:E_A]
    gc128 = _dot_01x(tril, g128)
    gce64 = _dot_x01(gc128, e64)
    gce128 = _dot_x01(gc128, e128)
    beta = _dot_x01(_sigmoid(pb), e128)
    egc = jnp.exp(gce128)
    kb4 = k4 * beta
    kbg4 = kb4 * egc
    vb4 = v4 * beta
    qg4 = q4 * egc

    gq = proj[:, E_GQ:E_GK] * GLA_DK ** -0.5
    gk = proj[:, E_GK:E_GV]
    gv = proj[:, E_GV:E_GR]
    gate = _mm(pb, w2p_ref[...]) + vec_ref[3:4, :]
    la = jnp.maximum(-_softplus(-gate) / GLA_TAU, GLA_LOG_GATE_MIN)
    lc = _dot_01x(tril, la)
    gqg = gq * jnp.exp(lc)
    gkg = gk * jnp.exp(-lc)

    pair_r = _iota2((2 * GDN_DK, 2 * GDN_DK), 0) >= GDN_DK
    pair_c = _iota2((2 * GDN_DK, 2 * GDN_DK), 1) >= GDN_DK
    gdn_bd = pair_r == pair_c
    gla_bd = (_iota2((GLA_HEADS * GLA_DV, GLA_HEADS * GLA_DK), 0) >> _log2(GLA_DV)) == \
             (_iota2((GLA_HEADS * GLA_DV, GLA_HEADS * GLA_DK), 1) >> _log2(GLA_DK))

    chunks = [slice(c * CHUNK, (c + 1) * CHUNK) for c in range(nc)]
    lasts = [slice((c + 1) * CHUNK - 1, (c + 1) * CHUNK) for c in range(nc)]

    dmask = [_decay_mask(gce64[cs], causal, eye, ones) for cs in chunks]
    la_ = [_mm_nt(jnp.concatenate([kb4[cs], q4[cs]], axis=0), _bd(k4[cs], blk128)) for cs in chunks]
    a4 = [m[CHUNK:] * d for m, d in zip(la_, dmask)]
    t4 = _tri_inv_many([jnp.where(strict, m[:CHUNK] * d, 0.0) for m, d in zip(la_, dmask)],
                       eye, levels, blk64)
    wu = [_mm(t, jnp.concatenate([_bd(kbg4[cs], blk128), _bd(vb4[cs], blk128)], axis=1))
          for t, cs in zip(t4, chunks)]
    aw = [_mm(a, jnp.concatenate([_bd(m[:, :MIX_W], blk128), _bd(m[:, MIX_W:], blk128)], axis=1))
          for a, m in zip(a4, wu)]
    att = [jnp.where(causal, _mm_nt(gqg[cs], _bd(gkg[cs], blk64)), 0.0) for cs in chunks]
    o_intra = [_mm(a, _bd(gv[cs], blk128)) for a, cs in zip(att, chunks)]

    for c, (cs, last) in enumerate(zip(chunks, lasts)):
        w4, u4 = wu[c][:, :MIX_W], wu[c][:, MIX_W:]
        qt4 = qg4[cs] - aw[c][:, :MIX_W]
        o0 = aw[c][:, MIX_W:]
        gcl = gce128[last]
        kd4 = k4[cs] * jnp.exp(gcl - gce128[cs])
        gl = jnp.exp(gcl)
        for p in range(GDN_HEADS // 2):
            psl = slice(p * 2 * GDN_DK, (p + 1) * 2 * GDN_DK)
            state = sgdn_s[p]
            res = _mm(jnp.concatenate([w4[:, psl], qt4[:, psl]], axis=0), state)
            v_new = u4[:, psl] - res[:CHUNK]
            oo_s[cs, psl] = o0[:, psl] + res[CHUNK:]
            sgdn_s[p] = jnp.where(gdn_bd, state * gl[:, psl] + _mm_tn(kd4[:, psl], v_new), 0.0)
        lcl = lc[last]
        st = sgla_s[...]
        oo_s[cs, MIX_W:] = o_intra[c] + _mm_nt(gqg[cs], st)
        kd = gk[cs] * jnp.exp(lcl - lc[cs])
        sgla_s[...] = jnp.where(gla_bd, st * jnp.exp(lcl) + _mm_tn(gv[cs], kd), 0.0)

    gdn_w = vec_ref[2:3, 0:LANES]
    gla_w = vec_ref[4:5, 0:LANES]
    for h in range(GDN_HEADS + GLA_HEADS):
        sl = slice(h * LANES, (h + 1) * LANES)
        if h < GDN_HEADS:
            gt, nw = proj[:, E_Z + h * LANES:E_Z + (h + 1) * LANES], gdn_w
        else:
            hh = h - GDN_HEADS
            gt, nw = proj[:, E_GR + hh * LANES:E_GR + (hh + 1) * LANES], gla_w
        oo_s[:, sl] = _rms(oo_s[:, sl], nw) * _silu(gt)
    o_ref[0] = x + jnp.dot(oo_s[...].astype(BF16), wout_ref[...], preferred_element_type=F32)


def _pad_cols(w, width, at=0):
    out = jnp.zeros((w.shape[0], width), w.dtype)
    return out.at[:, at:at + w.shape[1]].set(w)


def _expander(nblk, width):
    return (jnp.arange(LANES)[:, None] == (jnp.arange(nblk * width) // width)[None, :]).astype(BF16)


def _even_layer(h, nw, w_in, conv_w, a_log, dt_bias, gdn_norm_w, gla_w2, gla_b, gla_norm_w, w_out):
    bsz, seq, _ = h.shape
    rows = min(SEQ_BLOCK, seq)
    o = 0
    parts = []
    for s in (1536, 512, 4, 4, 256, 256, 512, 512, 16):
        parts.append(w_in[:, o:o + s])
        o += s
    w_qkv, w_z, w_b, w_a, w_gq, w_gk, w_gv, w_gr, w_glr = parts
    w_b128 = _pad_cols(w_b, LANES).at[:, E_GLR_LANE:E_GLR_LANE + GLA_GATE_RANK].set(w_glr)
    w_pack = jnp.concatenate([w_qkv, w_z, w_gq, w_gk, w_gv, w_gr, w_b128, _pad_cols(w_a, LANES)],
                             axis=1).astype(BF16)
    vec = jnp.zeros((8, 2 * LANES), F32)
    vec = vec.at[0, :GDN_HEADS].set(a_log).at[1, :GDN_HEADS].set(dt_bias)
    vec = vec.at[2, :LANES].set(gdn_norm_w).at[3, :].set(gla_b).at[4, :LANES].set(gla_norm_w)
    w2p = jnp.zeros((LANES, GLA_HEADS * GLA_DK), F32).at[E_GLR_LANE:E_GLR_LANE + GLA_GATE_RANK].set(gla_w2)

    def full(a):
        return pl.BlockSpec(a.shape, lambda b, j: (0,) * a.ndim)

    consts = (nw.reshape(1, D_MODEL), w_pack, conv_w, vec, w2p.astype(BF16),
              _expander(GROUP, CHUNK), _expander(GDN_HEADS, GDN_DK), w_out.astype(BF16))
    blk = pl.BlockSpec((1, rows, D_MODEL), lambda b, j: (b, j, 0))
    vm = lambda *s: pltpu.VMEM(s, F32)
    return pl.pallas_call(
        functools.partial(_even_kernel, rows=rows),
        grid=(bsz, seq // rows),
        in_specs=[blk] + [full(a) for a in consts],
        out_specs=blk,
        out_shape=jax.ShapeDtypeStruct(h.shape, F32),
        scratch_shapes=[
            vm(CARRY_ROWS, 3 * MIX_W), vm(GDN_HEADS // 2, 2 * GDN_DK, 2 * GDN_DK),
            vm(GLA_HEADS * GLA_DV, GLA_HEADS * GLA_DK), vm(rows, D_MODEL),
        ],
        compiler_params=pltpu.CompilerParams(
            dimension_semantics=("parallel", "arbitrary"), vmem_limit_bytes=VMEM_LIMIT),
        name="even_mixer",
    )(h, *consts)


O_Z, O_XBC, O_PD, O_DT, O_END = 0, 512, 1536, 3328, 3456
PD_W = 1792
C_X = SSD_HEADS * SSD_P
C_BC = SSD_GROUPS * SSD_STATE
D_HK = RWKV_HEADS * RWKV_DK


def _odd_kernel(x_ref, nw_ref, w_ref, cw_ref, cb_ref, mu_ref, vec_ref,
                w2p_ref, a2p_ref, g2_ref, exp_ref, bones_ref, wout_ref, o_ref,
                carry_s, pdc_s, sssd_s, srw_s, yy_s, *, rows):
    nc = rows // CHUNK

    @pl.when(pl.program_id(1) == 0)
    def _():
        carry_s[...] = jnp.zeros_like(carry_s)
        pdc_s[...] = jnp.zeros_like(pdc_s)
        sssd_s[...] = jnp.zeros_like(sssd_s)
        srw_s[...] = jnp.zeros_like(srw_s)

    x = x_ref[0]
    hn = _rms(x, nw_ref[...]).astype(BF16)
    proj = jnp.dot(hn, w_ref[...], preferred_element_type=F32)

    causal, strict, eye, levels, ones = _group_consts()
    tril = _chunk_tril(rows)
    blk64 = _lane_blocks(GROUP, CHUNK)
    expand = exp_ref[...]
    bones = bones_ref[...]

    z = proj[:, O_Z:O_XBC]
    xbc_raw = proj[:, O_XBC:O_PD]
    xbc = _silu(_causal_conv(xbc_raw, carry_s[...], cw_ref[...]) + cb_ref[...])
    carry_s[...] = xbc_raw[rows - CARRY_ROWS:, :]
    xs = xbc[:, :C_X]
    b_in = xbc[:, C_X:C_X + C_BC]
    c_in = xbc[:, C_X + C_BC:]
    dtb = vec_ref[0:1, 0:LANES]
    alog = vec_ref[1:2, 0:LANES]
    dt128 = _softplus(proj[:, O_DT:O_END] + dtb)
    ac128 = _dot_01x(tril, dt128 * -jnp.exp(alog))
    ace = _dot_x01(ac128, expand)
    xdt = xs * _dot_x01(dt128, expand)
    e_ac = jnp.exp(ace)

    pd = proj[:, O_PD:O_DT]
    pd = pd + (_shifted_rows(pd, pdc_s[...], 1) - pd) * mu_ref[...]
    pdc_s[...] = proj[rows - CARRY_ROWS:, O_PD:O_DT]
    r = pd[:, 0:D_HK]
    k = pd[:, D_HK:2 * D_HK]
    v = pd[:, 2 * D_HK:3 * D_HK]
    xwa = pd[:, 3 * D_HK:3 * D_HK + LANES]
    xg = pd[:, 3 * D_HK + LANES:]
    w0, a0 = vec_ref[2:3, :], vec_ref[3:4, :]
    k_k, k_a, r_k = vec_ref[4:5, :], vec_ref[5:6, :], vec_ref[6:7, :]
    w_log = -_softplus(-(w0 + _mm(jnp.tanh(xwa), w2p_ref[...]))) - 0.5
    lw = -jnp.exp(w_log)
    a = _sigmoid(a0 + _mm(xwa, a2p_ref[...]))
    g_out = _mm(_sigmoid(xg), g2_ref[...])
    kkr = k * k_k
    kk = kkr * lax.rsqrt(_dot_x01(kkr * kkr, bones, 2) + L2_EPS)
    k2 = k * (1.0 + (a - 1.0) * k_a)
    be = kk * a
    pc = _dot_01x(tril, lw)
    e_np = jnp.exp(-pc)
    a_hat = -kk * jnp.exp(pc - lw)
    b_hat = be * e_np
    k_hat = k2 * e_np
    r_hat = r * jnp.exp(pc)

    rw_bd = (_iota2((GROUP_W, GROUP_W), 0) >> _log2(RWKV_DK)) == (_iota2((GROUP_W, GROUP_W), 1) >> _log2(RWKV_DK))

    chunks = [slice(c * CHUNK, (c + 1) * CHUNK) for c in range(nc)]
    lasts = [slice((c + 1) * CHUNK - 1, (c + 1) * CHUNK) for c in range(nc)]
    groups = [slice(g * GROUP_W, (g + 1) * GROUP_W) for g in range(RWKV_HEADS // GROUP)]
    units = [(cs, gsl) for cs in chunks for gsl in groups]

    cbseg = []
    for cs in chunks:
        for g, gsl in enumerate(groups):
            ssl = slice(g * SSD_STATE, (g + 1) * SSD_STATE)
            cb4 = _mm_nt(c_in[cs, ssl], jnp.concatenate([b_in[cs, ssl].astype(BF16)] * GROUP, axis=0))
            cbseg.append(cb4 * _decay_mask(ace[cs, gsl], causal, eye, ones))
    y_intra = [_mm(m, _bd(xdt[cs, gsl], blk64)) for m, (cs, gsl) in zip(cbseg, units)]
    ar = [jnp.concatenate([a_hat[cs, gsl], r_hat[cs, gsl]], axis=0) for cs, gsl in units]
    mb = [_mm_nt(m, _bd(b_hat[cs, gsl], blk64)) for m, (cs, gsl) in zip(ar, units)]
    mk = [_mm_nt(m, _bd(k_hat[cs, gsl], blk64)) for m, (cs, gsl) in zip(ar, units)]
    m_rb = [jnp.where(causal, m[CHUNK:], 0.0) for m in mb]
    t4 = _tri_inv_many([jnp.where(strict, -m[:CHUNK], 0.0) for m in mb], eye, levels, blk64)
    mv = [_mm(jnp.concatenate([jnp.where(strict, m[:CHUNK], 0.0), jnp.where(causal, m[CHUNK:], 0.0)], axis=0),
              _bd(v[cs, gsl], blk64)) for m, (cs, gsl) in zip(mk, units)]
    tu = [_mm(t, jnp.concatenate([_bd(a_hat[cs, gsl], blk64), _bd(m[:CHUNK], blk64)], axis=1))
          for t, m, (cs, gsl) in zip(t4, mv, units)]
    ru = [_mm(m, jnp.concatenate([_bd(t[:, :GROUP_W], blk64), _bd(t[:, GROUP_W:], blk64)], axis=1))
          for m, t in zip(m_rb, tu)]

    for c, (cs, last) in enumerate(zip(chunks, lasts)):
        acl = ace[last]
        e_dec = jnp.exp(acl - ace[cs])
        cd = jnp.exp(acl)
        pcl = pc[last]
        e_dec_r = jnp.exp(pcl - pc[cs])
        b_til = be[cs] * e_dec_r
        k_til = k2[cs] * e_dec_r
        p_c = jnp.exp(pcl)
        for g, gsl in enumerate(groups):
            i = c * len(groups) + g
            ssl = slice(g * SSD_STATE, (g + 1) * SSD_STATE)
            state = sssd_s[g]
            yy_s[cs, gsl] = y_intra[i] + _mm(c_in[cs, ssl], state) * e_ac[cs, gsl]
            sssd_s[g] = state * cd[:, gsl] + _mm_tn(b_in[cs, ssl], xdt[cs, gsl] * e_dec[:, gsl])
            a_til, u0 = tu[i][:, :GROUP_W], tu[i][:, GROUP_W:]
            r_til = r_hat[cs, gsl] + ru[i][:, :GROUP_W]
            y0 = mv[i][CHUNK:] + ru[i][:, GROUP_W:]
            state = srw_s[g]
            res = _mm_nt(jnp.concatenate([a_til, r_til], axis=0), state)
            u = u0 + res[:CHUNK]
            yy_s[cs, C_X + g * GROUP_W:C_X + (g + 1) * GROUP_W] = y0 + res[CHUNK:]
            new = state * p_c[:, gsl] + _mm_tn(u, b_til[:, gsl]) + _mm_tn(v[cs, gsl], k_til[:, gsl])
            srw_s[g] = jnp.where(rw_bd, new, 0.0)

    d_skip = vec_ref[7:8, :]
    yc = (yy_s[:, 0:C_X] + xs * d_skip) * _silu(z)
    gw = C_X // SSD_GROUPS
    for g in range(SSD_GROUPS):
        gsl = slice(g * gw, (g + 1) * gw)
        yy_s[:, gsl] = _rms(yc[:, gsl], vec_ref[8:9, gsl])
    yd = yy_s[:, C_X:]
    inv_n = 1.0 / RWKV_DK
    mean = _dot_x01(yd, bones, 2) * inv_n
    cen = yd - mean
    var = _dot_x01(cen * cen, bones, 2) * inv_n
    yd = cen * lax.rsqrt(var + RWKV_GN_EPS) * vec_ref[9:10, :] + vec_ref[10:11, :]
    bonus = _dot_x01(r * k2 * r_k, bones, 2)
    yy_s[:, C_X:] = (yd + bonus * v) * g_out
    o_ref[0] = x + jnp.dot(yy_s[...].astype(BF16), wout_ref[...], preferred_element_type=F32)


def _odd_layer(h, nw, w_in, conv_w, conv_b, dt_bias, a_log, d_skip, ssd_norm_w, mu, w0, w2, a0, a2, g2,
               k_k, k_a, r_k, gn_w, gn_b, w_out):
    bsz, seq, _ = h.shape
    rows = min(SEQ_BLOCK, seq)
    w_z = w_in[:, 0:512]
    w_xbc = w_in[:, 512:1536]
    w_dt = w_in[:, 1536:1544]
    w_pd = w_in[:, 1544:1544 + PD_W]
    w_pack = jnp.concatenate([w_z, w_xbc, w_pd, _pad_cols(w_dt, LANES)], axis=1).astype(BF16)
    vec = jnp.zeros((16, D_HK), F32)
    vec = vec.at[0, :SSD_HEADS].set(dt_bias).at[1, :SSD_HEADS].set(a_log)
    vec = vec.at[2].set(w0).at[3].set(a0).at[4].set(k_k).at[5].set(k_a).at[6].set(r_k.reshape(-1))
    vec = vec.at[7].set(jnp.repeat(d_skip, SSD_P)).at[8].set(ssd_norm_w).at[9].set(gn_w).at[10].set(gn_b)
    rank = w2.shape[0]
    w2p = jnp.zeros((LANES, D_HK), F32).at[:rank].set(w2).astype(BF16)
    a2p = jnp.zeros((LANES, D_HK), F32).at[rank:rank + a2.shape[0]].set(a2).astype(BF16)
    head_of_lane = jnp.arange(D_HK) // RWKV_DK
    bones = (head_of_lane[:, None] == head_of_lane[None, :]).astype(BF16)

    def full(a):
        return pl.BlockSpec(a.shape, lambda b, j: (0,) * a.ndim)

    consts = (nw.reshape(1, D_MODEL), w_pack, conv_w, conv_b.reshape(1, -1), mu.reshape(1, PD_W),
              vec, w2p, a2p, g2.astype(BF16), _expander(SSD_HEADS, SSD_P), bones, w_out.astype(BF16))
    blk = pl.BlockSpec((1, rows, D_MODEL), lambda b, j: (b, j, 0))
    vm = lambda *s: pltpu.VMEM(s, F32)
    return pl.pallas_call(
        functools.partial(_odd_kernel, rows=rows),
        grid=(bsz, seq // rows),
        in_specs=[blk] + [full(a) for a in consts],
        out_specs=blk,
        out_shape=jax.ShapeDtypeStruct(h.shape, F32),
        scratch_shapes=[
            vm(CARRY_ROWS, C_X + 2 * C_BC), vm(CARRY_ROWS, PD_W),
            vm(SSD_GROUPS, SSD_STATE, GROUP_W), vm(RWKV_HEADS // GROUP, GROUP_W, GROUP_W),
            vm(rows, D_MODEL),
        ],
        compiler_params=pltpu.CompilerParams(
            dimension_semantics=("parallel", "arbitrary"), vmem_limit_bytes=VMEM_LIMIT),
        name="odd_mixer",
    )(h, *consts)


def kernel(x, norm_mix_w, norm_mlp_w, mlp_w1, mlp_w2, final_norm_w, even_w_in, gdn_conv_w, gdn_a_log, gdn_dt_bias, gdn_norm_w, gla_gate_w2, gla_gate_b, gla_norm_w, even_w_out, odd_w_in, ssd_conv_w, ssd_conv_b, ssd_dt_bias, ssd_a_log, ssd_d, ssd_norm_w, rwkv_mu, rwkv_w0, rwkv_w2, rwkv_a0, rwkv_a2, rwkv_g2, rwkv_k_k, rwkv_k_a, rwkv_r_k, rwkv_gn_w, rwkv_gn_b, odd_w_out):
    bsz, seq, _ = x.shape
    h = x
    for layer in range(DEPTH):
        i = layer // 2
        if layer % 2 == 0:
            h = _even_layer(h, norm_mix_w[layer], even_w_in[i], gdn_conv_w[i], gdn_a_log[i], gdn_dt_bias[i],
                            gdn_norm_w[i], gla_gate_w2[i], gla_gate_b[i], gla_norm_w[i], even_w_out[i])
        else:
            h = _odd_layer(h, norm_mix_w[layer], odd_w_in[i], ssd_conv_w[i], ssd_conv_b[i], ssd_dt_bias[i],
                           ssd_a_log[i], ssd_d[i], ssd_norm_w[i], rwkv_mu[i], rwkv_w0[i], rwkv_w2[i],
                           rwkv_a0[i], rwkv_a2[i], rwkv_g2[i], rwkv_k_k[i], rwkv_k_a[i], rwkv_r_k[i],
                           rwkv_gn_w[i], rwkv_gn_b[i], odd_w_out[i])
        last = layer == DEPTH - 1
        h = _mlp_layer(h.reshape(bsz * seq, D_MODEL), norm_mlp_w[layer], mlp_w1[layer], mlp_w2[layer],
                       final_norm_w, last).reshape(bsz, seq, D_MODEL)
    return h
```

```python
import functools

import jax
import jax.numpy as jnp
from jax import lax
from jax.experimental import pallas as pl
from jax.experimental.pallas import tpu as pltpu

F32 = jnp.float32
BF16 = jnp.bfloat16

D_MODEL = 1024
DEPTH = 4
CHUNK = 64
CONV_W = 4
D_FF = 4 * D_MODEL
NORM_EPS = 1e-6
L2_EPS = 1e-6
MIX_W = D_MODEL // 2
GDN_HEADS = 4
GDN_DK = 128
GLA_HEADS = 4
GLA_DK = 64
GLA_DV = 128
GLA_GATE_RANK = 16
GLA_TAU = 16.0
GLA_LOG_GATE_MIN = -1.0
SSD_HEADS = 8
SSD_P = 64
SSD_GROUPS = 2
SSD_STATE = 128
RWKV_HEADS = 8
RWKV_DK = 64
RWKV_GN_EPS = 64e-5

LANES = 128
CARRY_ROWS = 8
GROUP = 4
GROUP_W = GROUP * CHUNK
SEQ_BLOCK = 512
SEQS_PER_STEP = 1
N_EXP = 2
N_STAT = 1
MLP_ROWS = 512
MLP_FF_BLOCK = 1024
VMEM_LIMIT = 56 * 1024 * 1024


def _mm(a, b):
    return jnp.dot(a.astype(BF16), b.astype(BF16), preferred_element_type=F32)


def _mm_nt(a, b):
    return lax.dot_general(a.astype(BF16), b.astype(BF16), (((1,), (1,)), ((), ())),
                           preferred_element_type=F32)


def _mm_tn(a, b):
    return lax.dot_general(a.astype(BF16), b.astype(BF16), (((0,), (0,)), ((), ())),
                           preferred_element_type=F32)


def _split(x, n):
    pieces = []
    rem = x
    for i in range(n):
        p = rem.astype(BF16)
        pieces.append(p)
        if i + 1 < n:
            rem = rem - p.astype(F32)
    return pieces


def _dot_x01(x, m01, n):
    return sum(jnp.dot(p, m01, preferred_element_type=F32) for p in _split(x, n))


def _dot_01x(m01, x, n):
    return sum(jnp.dot(m01, p, preferred_element_type=F32) for p in _split(x, n))


def _rms(x, w, eps=NORM_EPS):
    return x * lax.rsqrt(jnp.mean(x * x, axis=-1, keepdims=True) + eps) * w


def _softplus(x):
    return jnp.maximum(x, 0.0) + jnp.log1p(jnp.exp(-jnp.abs(x)))


def _sigmoid(x):
    return 1.0 / (1.0 + jnp.exp(-x))


def _silu(x):
    return x * _sigmoid(x)


def _iota2(shape, dim):
    return lax.broadcasted_iota(jnp.int32, shape, dim)


def _log2(n):
    assert n & (n - 1) == 0
    return n.bit_length() - 1


def _lane_blocks(nblk, width):
    lane = _iota2((1, nblk * width), 1)
    return [(lane >= h * width) & (lane < (h + 1) * width) for h in range(nblk)]


def _bd(x, masks):
    return jnp.concatenate([jnp.where(m, x, 0.0).astype(BF16) for m in masks], axis=0)


def _group_consts():
    ii = _iota2((CHUNK, GROUP_W), 0)
    jj = _iota2((CHUNK, GROUP_W), 1) & (CHUNK - 1)
    causal = ii >= jj
    strict = ii > jj
    eye = jnp.where(ii == jj, 1.0, 0.0).astype(F32)
    levels = []
    k = 0
    while (1 << k) < CHUNK:
        same = (ii >> (k + 1)) == (jj >> (k + 1))
        lower = ((ii >> k) & 1) == 1
        left = ((jj >> k) & 1) == 0
        levels.append(jnp.where(same & lower & left, 1.0, 0.0).astype(F32))
        k += 1
    ones = jnp.ones((CHUNK, CHUNK), BF16)
    return causal, strict, eye, levels, ones


def _chunk_tril(rows):
    n = min(rows, GROUP_W)
    ii = _iota2((n, n), 0)
    jj = _iota2((n, n), 1)
    return jnp.where(((ii >> _log2(CHUNK)) == (jj >> _log2(CHUNK))) & (ii >= jj), 1.0, 0.0).astype(BF16)


def _chunk_cumsum(tril, x, n):
    t = tril.shape[0]
    return jnp.concatenate([_dot_01x(tril, x[i:i + t], n) for i in range(0, x.shape[0], t)], axis=0)


def _tri_inv_many(ls, eye, levels, blocks):
    xs = [eye - l * levels[0] for l in ls]
    for lvl in levels[1:]:
        ys = [_mm(x, _bd(l * lvl, blocks)) for x, l in zip(xs, ls)]
        xs = [x - _mm(y, _bd(x, blocks)) for x, y in zip(xs, ys)]
    return xs


def _rowcast(colx, eye, ones):
    return _dot_01x(ones, colx * eye, N_EXP)


def _decay_mask(colx, mask, eye, ones):
    return jnp.where(mask, jnp.exp(jnp.where(mask, colx - _rowcast(colx, eye, ones), 0.0)), 0.0)


def _shifted_rows(cur, carry, shift):
    full = jnp.concatenate([carry, cur], axis=0)
    if shift:
        full = pltpu.roll(full, shift, 0)
    return full[CARRY_ROWS:]


def _causal_conv(cur, carry, w):
    out = None
    for j in range(CONV_W):
        term = _shifted_rows(cur, carry, CONV_W - 1 - j) * w[j:j + 1, :]
        out = term if out is None else out + term
    return out


def _run_skewed(stage_gens):
    live = [True] * len(stage_gens)
    t = 0
    while any(live):
        for i, g in enumerate(stage_gens):
            if live[i] and t >= i:
                try:
                    next(g)
                except StopIteration:
                    live[i] = False
        t += 1


def _mlp_kernel(x_ref, nw_ref, w1_ref, w2_ref, fw_ref, o_ref, *, final_norm):
    x = x_ref[...]
    hn = _rms(x, nw_ref[...]).astype(BF16)
    acc = x
    for c in range(D_FF // MLP_FF_BLOCK):
        cols = slice(c * MLP_FF_BLOCK, (c + 1) * MLP_FF_BLOCK)
        h = jnp.dot(hn, w1_ref[:, cols], preferred_element_type=F32)
        h = jnp.square(jnp.maximum(h, 0.0)).astype(BF16)
        acc = acc + jnp.dot(h, w2_ref[cols, :], preferred_element_type=F32)
    if final_norm:
        acc = _rms(acc, fw_ref[...])
    o_ref[...] = acc


def _mlp_layer(h2d, nw, w1, w2, fw, final_norm):
    t = h2d.shape[0]
    tm = min(MLP_ROWS, t)
    const = lambda i: (0, 0)
    return pl.pallas_call(
        functools.partial(_mlp_kernel, final_norm=final_norm),
        grid=(t // tm,),
        in_specs=[
            pl.BlockSpec((tm, D_MODEL), lambda i: (i, 0)),
            pl.BlockSpec((1, D_MODEL), const),
            pl.BlockSpec((D_MODEL, D_FF), const),
            pl.BlockSpec((D_FF, D_MODEL), const),
            pl.BlockSpec((1, D_MODEL), const),
        ],
        out_specs=pl.BlockSpec((tm, D_MODEL), lambda i: (i, 0)),
        out_shape=jax.ShapeDtypeStruct((t, D_MODEL), F32),
        compiler_params=pltpu.CompilerParams(
            dimension_semantics=("parallel",), vmem_limit_bytes=VMEM_LIMIT),
        name="mlp_final" if final_norm else "mlp",
    )(h2d, nw.reshape(1, D_MODEL), w1.astype(BF16), w2.astype(BF16), fw.reshape(1, D_MODEL))


E_QKV, E_Z, E_GQ, E_GK, E_GV, E_GR, E_B, E_A, E_END = 0, 1536, 2048, 2304, 2560, 3072, 3584, 3712, 3840
E_GLR_LANE = 8


def _even_kernel(x_ref, nw_ref, w_ref, cw_ref, vec_ref, w2p_ref, e64_ref, e128_ref, wout_ref, o_ref,
                 carry_s, sgdn_s, sgla_s, oo_s, *, rows, nseq):
    nc = rows // CHUNK

    @pl.when(pl.program_id(1) == 0)
    def _():
        carry_s[...] = jnp.zeros_like(carry_s)
        sgdn_s[...] = jnp.zeros_like(sgdn_s)
        sgla_s[...] = jnp.zeros_like(sgla_s)

    causal, strict, eye, levels, ones = _group_consts()
    tril = _chunk_tril(rows)
    blk64 = _lane_blocks(GROUP, CHUNK)
    blk128 = _lane_blocks(GDN_HEADS, GDN_DK)
    e64, e128 = e64_ref[...], e128_ref[...]
    pair_r = _iota2((2 * GDN_DK, 2 * GDN_DK), 0) >= GDN_DK
    pair_c = _iota2((2 * GDN_DK, 2 * GDN_DK), 1) >= GDN_DK
    gdn_bd = pair_r == pair_c
    gla_bd = (_iota2((GLA_HEADS * GLA_DV, GLA_HEADS * GLA_DK), 0) >> _log2(GLA_DV)) == \
             (_iota2((GLA_HEADS * GLA_DV, GLA_HEADS * GLA_DK), 1) >> _log2(GLA_DK))
    chunks = [slice(c * CHUNK, (c + 1) * CHUNK) for c in range(nc)]
    lasts = [slice((c + 1) * CHUNK - 1, (c + 1) * CHUNK) for c in range(nc)]

    def stages(s):
        x = x_ref[s]
        hn = _rms(x, nw_ref[...]).astype(BF16)
        proj = jnp.dot(hn, w_ref[...], preferred_element_type=F32)
        qkv_raw = proj[:, E_QKV:E_Z]
        qkv = _silu(_causal_conv(qkv_raw, carry_s[s], cw_ref[...]))
        carry_s[s] = qkv_raw[rows - CARRY_ROWS:, :]
        qs, ks = [], []
        for h in range(GDN_HEADS):
            qh = qkv[:, h * GDN_DK:(h + 1) * GDN_DK]
            kh = qkv[:, MIX_W + h * GDN_DK:MIX_W + (h + 1) * GDN_DK]
            qs.append(qh * (lax.rsqrt(jnp.sum(qh * qh, axis=-1, keepdims=True) + L2_EPS) * GDN_DK ** -0.5))
            ks.append(kh * lax.rsqrt(jnp.sum(kh * kh, axis=-1, keepdims=True) + L2_EPS))
        q4 = jnp.concatenate(qs, axis=1)
        k4 = jnp.concatenate(ks, axis=1)
        v4 = qkv[:, 2 * MIX_W:]
        alog = vec_ref[0:1, 0:LANES]
        dtb = vec_ref[1:2, 0:LANES]
        g128 = -jnp.exp(alog) * _softplus(proj[:, E_A:E_END] + dtb)
        pb = proj[:, E_B:E_A]
        gc128 = _chunk_cumsum(tril,g128, N_EXP)
        gce64 = _dot_x01(gc128, e64, N_EXP)
        gce128 = _dot_x01(gc128, e128, N_EXP)
        beta = _dot_x01(_sigmoid(pb), e128, N_STAT)
        egc = jnp.exp(gce128)
        kb4 = k4 * beta
        kbg4 = kb4 * egc
        vb4 = v4 * beta
        qg4 = q4 * egc
        gq = proj[:, E_GQ:E_GK] * GLA_DK ** -0.5
        gk = proj[:, E_GK:E_GV]
        gv = proj[:, E_GV:E_GR]
        gate = _mm(pb, w2p_ref[...]) + vec_ref[3:4, :]
        la = jnp.maximum(-_softplus(-gate) / GLA_TAU, GLA_LOG_GATE_MIN)
        lc = _chunk_cumsum(tril,la, N_EXP)
        gqg = gq * jnp.exp(lc)
        gkg = gk * jnp.exp(-lc)
        yield

        dmask = [_decay_mask(gce64[cs], causal, eye, ones) for cs in chunks]
        la_ = [_mm_nt(jnp.concatenate([kb4[cs], q4[cs]], axis=0), _bd(k4[cs], blk128)) for cs in chunks]
        a4 = [m[CHUNK:] * d for m, d in zip(la_, dmask)]
        t4 = _tri_inv_many([jnp.where(strict, m[:CHUNK] * d, 0.0) for m, d in zip(la_, dmask)],
                           eye, levels, blk64)
        wu = [_mm(t, jnp.concatenate([_bd(kbg4[cs], blk128), _bd(vb4[cs], blk128)], axis=1))
              for t, cs in zip(t4, chunks)]
        aw = [_mm(a, jnp.concatenate([_bd(m[:, :MIX_W], blk128), _bd(m[:, MIX_W:], blk128)], axis=1))
              for a, m in zip(a4, wu)]
        att = [jnp.where(causal, _mm_nt(gqg[cs], _bd(gkg[cs], blk64)), 0.0) for cs in chunks]
        o_intra = [_mm(a, _bd(gv[cs], blk128)) for a, cs in zip(att, chunks)]
        yield

        for c, (cs, last) in enumerate(zip(chunks, lasts)):
            w4, u4 = wu[c][:, :MIX_W], wu[c][:, MIX_W:]
            qt4 = qg4[cs] - aw[c][:, :MIX_W]
            o0 = aw[c][:, MIX_W:]
            gcl = gce128[last]
            kd4 = k4[cs] * jnp.exp(gcl - gce128[cs])
            gl = jnp.exp(gcl)
            for p in range(GDN_HEADS // 2):
                psl = slice(p * 2 * GDN_DK, (p + 1) * 2 * GDN_DK)
                state = sgdn_s[s, p]
                res = _mm(jnp.concatenate([w4[:, psl], qt4[:, psl]], axis=0), state)
                v_new = u4[:, psl] - res[:CHUNK]
                oo_s[s, cs, psl] = o0[:, psl] + res[CHUNK:]
                sgdn_s[s, p] = jnp.where(gdn_bd, state * gl[:, psl] + _mm_tn(kd4[:, psl], v_new), 0.0)
            lcl = lc[last]
            st = sgla_s[s]
            oo_s[s, cs, MIX_W:] = o_intra[c] + _mm_nt(gqg[cs], st)
            kd = gk[cs] * jnp.exp(lcl - lc[cs])
            sgla_s[s] = jnp.where(gla_bd, st * jnp.exp(lcl) + _mm_tn(gv[cs], kd), 0.0)
        yield

        gdn_w = vec_ref[2:3, 0:LANES]
        gla_w = vec_ref[4:5, 0:LANES]
        for h in range(GDN_HEADS + GLA_HEADS):
            sl = slice(h * LANES, (h + 1) * LANES)
            if h < GDN_HEADS:
                gt, nw = proj[:, E_Z + h * LANES:E_Z + (h + 1) * LANES], gdn_w
            else:
                hh = h - GDN_HEADS
                gt, nw = proj[:, E_GR + hh * LANES:E_GR + (hh + 1) * LANES], gla_w
            oo_s[s, :, sl] = _rms(oo_s[s, :, sl], nw) * _silu(gt)
        o_ref[s] = x + jnp.dot(oo_s[s].astype(BF16), wout_ref[...], preferred_element_type=F32)

    _run_skewed([stages(s) for s in range(nseq)])


def _pad_cols(w, width, at=0):
    out = jnp.zeros((w.shape[0], width), w.dtype)
    return out.at[:, at:at + w.shape[1]].set(w)


def _expander(nblk, width):
    return (jnp.arange(LANES)[:, None] == (jnp.arange(nblk * width) // width)[None, :]).astype(BF16)


def _mixer_call(body, h, consts, scratch, name):
    bsz, seq, _ = h.shape
    rows = min(SEQ_BLOCK, seq)
    nseq = SEQS_PER_STEP if bsz % SEQS_PER_STEP == 0 else 1

    def full(a):
        return pl.BlockSpec(a.shape, lambda b, j: (0,) * a.ndim)

    blk = pl.BlockSpec((nseq, rows, D_MODEL), lambda b, j: (b, j, 0))
    return pl.pallas_call(
        functools.partial(body, rows=rows, nseq=nseq),
        grid=(bsz // nseq, seq // rows),
        in_specs=[blk] + [full(a) for a in consts],
        out_specs=blk,
        out_shape=jax.ShapeDtypeStruct(h.shape, F32),
        scratch_shapes=[pltpu.VMEM((nseq,) + s, F32) for s in scratch(rows)],
        compiler_params=pltpu.CompilerParams(
            dimension_semantics=("parallel", "arbitrary"), vmem_limit_bytes=VMEM_LIMIT),
        name=name,
    )(h, *consts)


def _even_layer(h, nw, w_in, conv_w, a_log, dt_bias, gdn_norm_w, gla_w2, gla_b, gla_norm_w, w_out):
    o = 0
    parts = []
    for s in (1536, 512, 4, 4, 256, 256, 512, 512, 16):
        parts.append(w_in[:, o:o + s])
        o += s
    w_qkv, w_z, w_b, w_a, w_gq, w_gk, w_gv, w_gr, w_glr = parts
    w_b128 = _pad_cols(w_b, LANES).at[:, E_GLR_LANE:E_GLR_LANE + GLA_GATE_RANK].set(w_glr)
    w_pack = jnp.concatenate([w_qkv, w_z, w_gq, w_gk, w_gv, w_gr, w_b128, _pad_cols(w_a, LANES)],
                             axis=1).astype(BF16)
    vec = jnp.zeros((8, 2 * LANES), F32)
    vec = vec.at[0, :GDN_HEADS].set(a_log).at[1, :GDN_HEADS].set(dt_bias)
    vec = vec.at[2, :LANES].set(gdn_norm_w).at[3, :].set(gla_b).at[4, :LANES].set(gla_norm_w)
    w2p = jnp.zeros((LANES, GLA_HEADS * GLA_DK), F32).at[E_GLR_LANE:E_GLR_LANE + GLA_GATE_RANK].set(gla_w2)
    consts = (nw.reshape(1, D_MODEL), w_pack, conv_w, vec, w2p.astype(BF16),
              _expander(GROUP, CHUNK), _expander(GDN_HEADS, GDN_DK), w_out.astype(BF16))
    scratch = lambda rows: [(CARRY_ROWS, 3 * MIX_W), (GDN_HEADS // 2, 2 * GDN_DK, 2 * GDN_DK),
                            (GLA_HEADS * GLA_DV, GLA_HEADS * GLA_DK), (rows, D_MODEL)]
    return _mixer_call(_even_kernel, h, consts, scratch, "even_mixer")


O_Z, O_XBC, O_PD, O_DT, O_END = 0, 512, 1536, 3328, 3456
PD_W = 1792
C_X = SSD_HEADS * SSD_P
C_BC = SSD_GROUPS * SSD_STATE
D_HK = RWKV_HEADS * RWKV_DK


def _odd_kernel(x_ref, nw_ref, w_ref, cw_ref, cb_ref, mu_ref, vec_ref,
                w2p_ref, a2p_ref, g2_ref, exp_ref, bones_ref, wout_ref, o_ref,
                carry_s, pdc_s, sssd_s, srw_s, yy_s, *, rows, nseq):
    nc = rows // CHUNK

    @pl.when(pl.program_id(1) == 0)
    def _():
        carry_s[...] = jnp.zeros_like(carry_s)
        pdc_s[...] = jnp.zeros_like(pdc_s)
        sssd_s[...] = jnp.zeros_like(sssd_s)
        srw_s[...] = jnp.zeros_like(srw_s)

    causal, strict, eye, levels, ones = _group_consts()
    tril = _chunk_tril(rows)
    blk64 = _lane_blocks(GROUP, CHUNK)
    expand = exp_ref[...]
    bones = bones_ref[...]
    rw_bd = (_iota2((GROUP_W, GROUP_W), 0) >> _log2(RWKV_DK)) == (_iota2((GROUP_W, GROUP_W), 1) >> _log2(RWKV_DK))
    chunks = [slice(c * CHUNK, (c + 1) * CHUNK) for c in range(nc)]
    lasts = [slice((c + 1) * CHUNK - 1, (c + 1) * CHUNK) for c in range(nc)]
    groups = [slice(g * GROUP_W, (g + 1) * GROUP_W) for g in range(RWKV_HEADS // GROUP)]
    units = [(cs, gsl) for cs in chunks for gsl in groups]

    def stages(s):
        x = x_ref[s]
        hn = _rms(x, nw_ref[...]).astype(BF16)
        proj = jnp.dot(hn, w_ref[...], preferred_element_type=F32)
        z = proj[:, O_Z:O_XBC]
        xbc_raw = proj[:, O_XBC:O_PD]
        xbc = _silu(_causal_conv(xbc_raw, carry_s[s], cw_ref[...]) + cb_ref[...])
        carry_s[s] = xbc_raw[rows - CARRY_ROWS:, :]
        xs = xbc[:, :C_X]
        b_in = xbc[:, C_X:C_X + C_BC]
        c_in = xbc[:, C_X + C_BC:]
        dtb = vec_ref[0:1, 0:LANES]
        alog = vec_ref[1:2, 0:LANES]
        dt128 = _softplus(proj[:, O_DT:O_END] + dtb)
        ac128 = _chunk_cumsum(tril,dt128 * -jnp.exp(alog), N_EXP)
        ace = _dot_x01(ac128, expand, N_EXP)
        xdt = xs * _dot_x01(dt128, expand, N_STAT)
        e_ac = jnp.exp(ace)
        pd = proj[:, O_PD:O_DT]
        pd = pd + (_shifted_rows(pd, pdc_s[s], 1) - pd) * mu_ref[...]
        pdc_s[s] = proj[rows - CARRY_ROWS:, O_PD:O_DT]
        r = pd[:, 0:D_HK]
        k = pd[:, D_HK:2 * D_HK]
        v = pd[:, 2 * D_HK:3 * D_HK]
        xwa = pd[:, 3 * D_HK:3 * D_HK + LANES]
        xg = pd[:, 3 * D_HK + LANES:]
        w0, a0 = vec_ref[2:3, :], vec_ref[3:4, :]
        k_k, k_a, r_k = vec_ref[4:5, :], vec_ref[5:6, :], vec_ref[6:7, :]
        w_log = -_softplus(-(w0 + _mm(jnp.tanh(xwa), w2p_ref[...]))) - 0.5
        lw = -jnp.exp(w_log)
        a = _sigmoid(a0 + _mm(xwa, a2p_ref[...]))
        g_out = _mm(_sigmoid(xg), g2_ref[...])
        kkr = k * k_k
        kk = kkr * lax.rsqrt(_dot_x01(kkr * kkr, bones, N_STAT) + L2_EPS)
        k2 = k * (1.0 + (a - 1.0) * k_a)
        be = kk * a
        pc = _chunk_cumsum(tril,lw, N_EXP)
        e_np = jnp.exp(-pc)
        a_hat = -kk * jnp.exp(pc - lw)
        b_hat = be * e_np
        k_hat = k2 * e_np
        r_hat = r * jnp.exp(pc)
        yield

        cbseg = []
        for cs in chunks:
            for g, gsl in enumerate(groups):
                ssl = slice(g * SSD_STATE, (g + 1) * SSD_STATE)
                cb4 = _mm_nt(c_in[cs, ssl], jnp.concatenate([b_in[cs, ssl].astype(BF16)] * GROUP, axis=0))
                cbseg.append(cb4 * _decay_mask(ace[cs, gsl], causal, eye, ones))
        y_intra = [_mm(m, _bd(xdt[cs, gsl], blk64)) for m, (cs, gsl) in zip(cbseg, units)]
        ar = [jnp.concatenate([a_hat[cs, gsl], r_hat[cs, gsl]], axis=0) for cs, gsl in units]
        mb = [_mm_nt(m, _bd(b_hat[cs, gsl], blk64)) for m, (cs, gsl) in zip(ar, units)]
        mk = [_mm_nt(m, _bd(k_hat[cs, gsl], blk64)) for m, (cs, gsl) in zip(ar, units)]
        m_rb = [jnp.where(causal, m[CHUNK:], 0.0) for m in mb]
        t4 = _tri_inv_many([jnp.where(strict, -m[:CHUNK], 0.0) for m in mb], eye, levels, blk64)
        mv = [_mm(jnp.concatenate([jnp.where(strict, m[:CHUNK], 0.0), jnp.where(causal, m[CHUNK:], 0.0)], axis=0),
                  _bd(v[cs, gsl], blk64)) for m, (cs, gsl) in zip(mk, units)]
        tu = [_mm(t, jnp.concatenate([_bd(a_hat[cs, gsl], blk64), _bd(m[:CHUNK], blk64)], axis=1))
              for t, m, (cs, gsl) in zip(t4, mv, units)]
        ru = [_mm(m, jnp.concatenate([_bd(t[:, :GROUP_W], blk64), _bd(t[:, GROUP_W:], blk64)], axis=1))
              for m, t in zip(m_rb, tu)]
        yield

        for c, (cs, last) in enumerate(zip(chunks, lasts)):
            acl = ace[last]
            e_dec = jnp.exp(acl - ace[cs])
            cd = jnp.exp(acl)
            pcl = pc[last]
            e_dec_r = jnp.exp(pcl - pc[cs])
            b_til = be[cs] * e_dec_r
            k_til = k2[cs] * e_dec_r
            p_c = jnp.exp(pcl)
            for g, gsl in enumerate(groups):
                i = c * len(groups) + g
                ssl = slice(g * SSD_STATE, (g + 1) * SSD_STATE)
                state = sssd_s[s, g]
                yy_s[s, cs, gsl] = y_intra[i] + _mm(c_in[cs, ssl], state) * e_ac[cs, gsl]
                sssd_s[s, g] = state * cd[:, gsl] + _mm_tn(b_in[cs, ssl], xdt[cs, gsl] * e_dec[:, gsl])
                a_til, u0 = tu[i][:, :GROUP_W], tu[i][:, GROUP_W:]
                r_til = r_hat[cs, gsl] + ru[i][:, :GROUP_W]
                y0 = mv[i][CHUNK:] + ru[i][:, GROUP_W:]
                state = srw_s[s, g]
                res = _mm_nt(jnp.concatenate([a_til, r_til], axis=0), state)
                u = u0 + res[:CHUNK]
                yy_s[s, cs, C_X + g * GROUP_W:C_X + (g + 1) * GROUP_W] = y0 + res[CHUNK:]
                new = state * p_c[:, gsl] + _mm_tn(jnp.concatenate([u, v[cs, gsl]], axis=0),
                                                   jnp.concatenate([b_til[:, gsl], k_til[:, gsl]], axis=0))
                srw_s[s, g] = jnp.where(rw_bd, new, 0.0)
        yield

        d_skip = vec_ref[7:8, :]
        yc = (yy_s[s, :, 0:C_X] + xs * d_skip) * _silu(z)
        gw = C_X // SSD_GROUPS
        for g in range(SSD_GROUPS):
            gsl = slice(g * gw, (g + 1) * gw)
            yy_s[s, :, gsl] = _rms(yc[:, gsl], vec_ref[8:9, gsl])
        yd = yy_s[s, :, C_X:]
        inv_n = 1.0 / RWKV_DK
        mean = _dot_x01(yd, bones, N_STAT) * inv_n
        cen = yd - mean
        var = _dot_x01(cen * cen, bones, N_STAT) * inv_n
        yd = cen * lax.rsqrt(var + RWKV_GN_EPS) * vec_ref[9:10, :] + vec_ref[10:11, :]
        bonus = _dot_x01(r * k2 * r_k, bones, N_STAT)
        yy_s[s, :, C_X:] = (yd + bonus * v) * g_out
        o_ref[s] = x + jnp.dot(yy_s[s].astype(BF16), wout_ref[...], preferred_element_type=F32)

    _run_skewed([stages(s) for s in range(nseq)])


def _odd_layer(h, nw, w_in, conv_w, conv_b, dt_bias, a_log, d_skip, ssd_norm_w, mu, w0, w2, a0, a2, g2,
               k_k, k_a, r_k, gn_w, gn_b, w_out):
    w_z = w_in[:, 0:512]
    w_xbc = w_in[:, 512:1536]
    w_dt = w_in[:, 1536:1544]
    w_pd = w_in[:, 1544:1544 + PD_W]
    w_pack = jnp.concatenate([w_z, w_xbc, w_pd, _pad_cols(w_dt, LANES)], axis=1).astype(BF16)
    vec = jnp.zeros((16, D_HK), F32)
    vec = vec.at[0, :SSD_HEADS].set(dt_bias).at[1, :SSD_HEADS].set(a_log)
    vec = vec.at[2].set(w0).at[3].set(a0).at[4].set(k_k).at[5].set(k_a).at[6].set(r_k.reshape(-1))
    vec = vec.at[7].set(jnp.repeat(d_skip, SSD_P)).at[8].set(ssd_norm_w).at[9].set(gn_w).at[10].set(gn_b)
    rank = w2.shape[0]
    w2p = jnp.zeros((LANES, D_HK), F32).at[:rank].set(w2).astype(BF16)
    a2p = jnp.zeros((LANES, D_HK), F32).at[rank:rank + a2.shape[0]].set(a2).astype(BF16)
    head_of_lane = jnp.arange(D_HK) // RWKV_DK
    bones = (head_of_lane[:, None] == head_of_lane[None, :]).astype(BF16)
    consts = (nw.reshape(1, D_MODEL), w_pack, conv_w, conv_b.reshape(1, -1), mu.reshape(1, PD_W),
              vec, w2p, a2p, g2.astype(BF16), _expander(SSD_HEADS, SSD_P), bones, w_out.astype(BF16))
    scratch = lambda rows: [(CARRY_ROWS, C_X + 2 * C_BC), (CARRY_ROWS, PD_W),
                            (SSD_GROUPS, SSD_STATE, GROUP_W), (RWKV_HEADS // GROUP, GROUP_W, GROUP_W),
                            (rows, D_MODEL)]
    return _mixer_call(_odd_kernel, h, consts, scratch, "odd_mixer")


def kernel(x, norm_mix_w, norm_mlp_w, mlp_w1, mlp_w2, final_norm_w, even_w_in, gdn_conv_w, gdn_a_log, gdn_dt_bias, gdn_norm_w, gla_gate_w2, gla_gate_b, gla_norm_w, even_w_out, odd_w_in, ssd_conv_w, ssd_conv_b, ssd_dt_bias, ssd_a_log, ssd_d, ssd_norm_w, rwkv_mu, rwkv_w0, rwkv_w2, rwkv_a0, rwkv_a2, rwkv_g2, rwkv_k_k, rwkv_k_a, rwkv_r_k, rwkv_gn_w, rwkv_gn_b, odd_w_out):
    bsz, seq, _ = x.shape
    h = x
    for layer in range(DEPTH):
        i = layer // 2
        if layer % 2 == 0:
            h = _even_layer(h, norm_mix_w[layer], even_w_in[i], gdn_conv_w[i], gdn_a_log[i], gdn_dt_bias[i],
                            gdn_norm_w[i], gla_gate_w2[i], gla_gate_b[i], gla_norm_w[i], even_w_out[i])
        else:
            h = _odd_layer(h, norm_mix_w[layer], odd_w_in[i], ssd_conv_w[i], ssd_conv_b[i], ssd_dt_bias[i],
                           ssd_a_log[i], ssd_d[i], ssd_norm_w[i], rwkv_mu[i], rwkv_w0[i], rwkv_w2[i],
                           rwkv_a0[i], rwkv_a2[i], rwkv_g2[i], rwkv_k_k[i], rwkv_k_a[i], rwkv_r_k[i],
                           rwkv_gn_w[i], rwkv_gn_b[i], odd_w_out[i])
        last = layer == DEPTH - 1
        h = _mlp_layer(h.reshape(bsz * seq, D_MODEL), norm_mlp_w[layer], mlp_w1[layer], mlp_w2[layer],
                       final_norm_w, last).reshape(bsz, seq, D_MODEL)
    return h
```

```python
import functools

import jax
import jax.numpy as jnp
from jax import lax
from jax.experimental import pallas as pl
from jax.experimental.pallas import tpu as pltpu

F32 = jnp.float32
BF16 = jnp.bfloat16

D_MODEL = 1024
DEPTH = 4
CHUNK = 64
CONV_W = 4
D_FF = 4 * D_MODEL
NORM_EPS = 1e-6
L2_EPS = 1e-6
MIX_W = D_MODEL // 2
GDN_HEADS = 4
GDN_DK = 128
GLA_HEADS = 4
GLA_DK = 64
GLA_DV = 128
GLA_GATE_RANK = 16
GLA_TAU = 16.0
GLA_LOG_GATE_MIN = -1.0
SSD_HEADS = 8
SSD_P = 64
SSD_GROUPS = 2
SSD_STATE = 128
RWKV_HEADS = 8
RWKV_DK = 64
RWKV_GN_EPS = 64e-5

LANES = 128
CARRY_ROWS = 8
GROUP = 4
GROUP_W = GROUP * CHUNK
SEQ_BLOCK = 512
SEQS_PER_STEP = 1
N_EXP = 2
N_STAT = 1
MLP_ROWS = 1024
MLP_FF_BLOCK = 1024
VMEM_LIMIT = 56 * 1024 * 1024


def _mm(a, b):
    return jnp.dot(a.astype(BF16), b.astype(BF16), preferred_element_type=F32)


def _mm_nt(a, b):
    return lax.dot_general(a.astype(BF16), b.astype(BF16), (((1,), (1,)), ((), ())),
                           preferred_element_type=F32)


def _mm_tn(a, b):
    return lax.dot_general(a.astype(BF16), b.astype(BF16), (((0,), (0,)), ((), ())),
                           preferred_element_type=F32)


def _split(x, n):
    pieces = []
    rem = x
    for i in range(n):
        p = rem.astype(BF16)
        pieces.append(p)
        if i + 1 < n:
            rem = rem - p.astype(F32)
    return pieces


def _dot_x01(x, m01, n):
    return sum(jnp.dot(p, m01, preferred_element_type=F32) for p in _split(x, n))


def _dot_01x(m01, x, n):
    return sum(jnp.dot(m01, p, preferred_element_type=F32) for p in _split(x, n))


def _rms(x, w, eps=NORM_EPS):
    return x * lax.rsqrt(jnp.mean(x * x, axis=-1, keepdims=True) + eps) * w


def _softplus(x):
    return jnp.maximum(x, 0.0) + jnp.log1p(jnp.exp(-jnp.abs(x)))


def _sigmoid(x):
    return 0.5 * jnp.tanh(0.5 * x) + 0.5


def _silu(x):
    return x * _sigmoid(x)


def _iota2(shape, dim):
    return lax.broadcasted_iota(jnp.int32, shape, dim)


def _log2(n):
    assert n & (n - 1) == 0
    return n.bit_length() - 1


def _lane_blocks(nblk, width):
    lane = _iota2((1, nblk * width), 1)
    return [(lane >= h * width) & (lane < (h + 1) * width) for h in range(nblk)]


def _bd(x, masks):
    return jnp.concatenate([jnp.where(m, x, 0.0).astype(BF16) for m in masks], axis=0)


def _group_consts():
    ii = _iota2((CHUNK, GROUP_W), 0)
    jj = _iota2((CHUNK, GROUP_W), 1) & (CHUNK - 1)
    causal = ii >= jj
    strict = ii > jj
    eye = jnp.where(ii == jj, 1.0, 0.0).astype(F32)
    levels = []
    k = 0
    while (1 << k) < CHUNK:
        same = (ii >> (k + 1)) == (jj >> (k + 1))
        lower = ((ii >> k) & 1) == 1
        left = ((jj >> k) & 1) == 0
        levels.append(jnp.where(same & lower & left, 1.0, 0.0).astype(F32))
        k += 1
    ones = jnp.ones((CHUNK, CHUNK), BF16)
    return causal, strict, eye, levels, ones


def _chunk_tril(rows):
    n = min(rows, GROUP_W)
    ii = _iota2((n, n), 0)
    jj = _iota2((n, n), 1)
    return jnp.where(((ii >> _log2(CHUNK)) == (jj >> _log2(CHUNK))) & (ii >= jj), 1.0, 0.0).astype(BF16)


def _chunk_cumsum(tril, x, n):
    t = tril.shape[0]
    return jnp.concatenate([_dot_01x(tril, x[i:i + t], n) for i in range(0, x.shape[0], t)], axis=0)


def _tri_inv_many(ls, eye, levels, blocks):
    xs = [eye - l * levels[0] for l in ls]
    for lvl in levels[1:]:
        ys = [_mm(x, _bd(l * lvl, blocks)) for x, l in zip(xs, ls)]
        xs = [x - _mm(y, _bd(x, blocks)) for x, y in zip(xs, ys)]
    return xs


def _rowcast(colx, eye, ones):
    return _dot_01x(ones, colx * eye, N_EXP)


def _decay_mask(colx, mask, eye, ones):
    return jnp.where(mask, jnp.exp(jnp.where(mask, colx - _rowcast(colx, eye, ones), 0.0)), 0.0)


def _shifted_rows(cur, carry, shift):
    full = jnp.concatenate([carry, cur], axis=0)
    if shift:
        full = pltpu.roll(full, shift, 0)
    return full[CARRY_ROWS:]


def _causal_conv(cur, carry, w):
    out = None
    for j in range(CONV_W):
        term = _shifted_rows(cur, carry, CONV_W - 1 - j) * w[j:j + 1, :]
        out = term if out is None else out + term
    return out


def _run_skewed(stage_gens):
    live = [True] * len(stage_gens)
    t = 0
    while any(live):
        for i, g in enumerate(stage_gens):
            if live[i] and t >= i:
                try:
                    next(g)
                except StopIteration:
                    live[i] = False
        t += 1


def _mlp_kernel(x_ref, nw_ref, w1_ref, w2_ref, fw_ref, o_ref, *, final_norm):
    x = x_ref[...]
    hn = _rms(x, nw_ref[...]).astype(BF16)
    acc = x
    for c in range(D_FF // MLP_FF_BLOCK):
        cols = slice(c * MLP_FF_BLOCK, (c + 1) * MLP_FF_BLOCK)
        h = jnp.dot(hn, w1_ref[:, cols], preferred_element_type=F32)
        h = jnp.square(jnp.maximum(h, 0.0)).astype(BF16)
        acc = acc + jnp.dot(h, w2_ref[cols, :], preferred_element_type=F32)
    if final_norm:
        acc = _rms(acc, fw_ref[...])
    o_ref[...] = acc


def _mlp_layer(h2d, nw, w1, w2, fw, final_norm):
    t = h2d.shape[0]
    tm = min(MLP_ROWS, t)
    const = lambda i: (0, 0)
    return pl.pallas_call(
        functools.partial(_mlp_kernel, final_norm=final_norm),
        grid=(t // tm,),
        in_specs=[
            pl.BlockSpec((tm, D_MODEL), lambda i: (i, 0)),
            pl.BlockSpec((1, D_MODEL), const),
            pl.BlockSpec((D_MODEL, D_FF), const),
            pl.BlockSpec((D_FF, D_MODEL), const),
            pl.BlockSpec((1, D_MODEL), const),
        ],
        out_specs=pl.BlockSpec((tm, D_MODEL), lambda i: (i, 0)),
        out_shape=jax.ShapeDtypeStruct((t, D_MODEL), F32),
        compiler_params=pltpu.CompilerParams(
            dimension_semantics=("parallel",), vmem_limit_bytes=VMEM_LIMIT),
        name="mlp_final" if final_norm else "mlp",
    )(h2d, nw.reshape(1, D_MODEL), w1.astype(BF16), w2.astype(BF16), fw.reshape(1, D_MODEL))


E_A, E_B, E_GK, E_GQ, E_QKV, E_GV, E_GR, E_Z, E_END = 0, 128, 256, 512, 768, 2304, 2816, 3328, 3840
E_GLR_LANE = 8


def _even_kernel(x_ref, nw_ref, w_ref, cw_ref, vec_ref, w2p_ref, e64_ref, e128_ref, wout_ref, o_ref,
                 carry_s, sgdn_s, sgla_s, oo_s, *, rows, nseq):
    nc = rows // CHUNK

    @pl.when(pl.program_id(1) == 0)
    def _():
        carry_s[...] = jnp.zeros_like(carry_s)
        sgdn_s[...] = jnp.zeros_like(sgdn_s)
        sgla_s[...] = jnp.zeros_like(sgla_s)

    causal, strict, eye, levels, ones = _group_consts()
    tril = _chunk_tril(rows)
    blk64 = _lane_blocks(GROUP, CHUNK)
    blk128 = _lane_blocks(GDN_HEADS, GDN_DK)
    e64, e128 = e64_ref[...], e128_ref[...]
    pair_r = _iota2((2 * GDN_DK, 2 * GDN_DK), 0) >= GDN_DK
    pair_c = _iota2((2 * GDN_DK, 2 * GDN_DK), 1) >= GDN_DK
    gdn_bd = pair_r == pair_c
    gla_bd = (_iota2((GLA_HEADS * GLA_DV, GLA_HEADS * GLA_DK), 0) >> _log2(GLA_DV)) == \
             (_iota2((GLA_HEADS * GLA_DV, GLA_HEADS * GLA_DK), 1) >> _log2(GLA_DK))
    chunks = [slice(c * CHUNK, (c + 1) * CHUNK) for c in range(nc)]
    lasts = [slice((c + 1) * CHUNK - 1, (c + 1) * CHUNK) for c in range(nc)]

    def stages(s):
        x = x_ref[s]
        hn = _rms(x, nw_ref[...]).astype(BF16)
        proj = jnp.dot(hn, w_ref[...], preferred_element_type=F32)
        qkv_raw = proj[:, E_QKV:E_QKV + 3 * MIX_W]
        qkv = _silu(_causal_conv(qkv_raw, carry_s[s], cw_ref[...]))
        carry_s[s] = qkv_raw[rows - CARRY_ROWS:, :]
        qs, ks = [], []
        for h in range(GDN_HEADS):
            qh = qkv[:, h * GDN_DK:(h + 1) * GDN_DK]
            kh = qkv[:, MIX_W + h * GDN_DK:MIX_W + (h + 1) * GDN_DK]
            qs.append(qh * (lax.rsqrt(jnp.sum(qh * qh, axis=-1, keepdims=True) + L2_EPS) * GDN_DK ** -0.5))
            ks.append(kh * lax.rsqrt(jnp.sum(kh * kh, axis=-1, keepdims=True) + L2_EPS))
        q4 = jnp.concatenate(qs, axis=1)
        k4 = jnp.concatenate(ks, axis=1)
        v4 = qkv[:, 2 * MIX_W:]
        alog = vec_ref[0:1, 0:LANES]
        dtb = vec_ref[1:2, 0:LANES]
        g128 = -jnp.exp(alog) * _softplus(proj[:, E_A:E_A + LANES] + dtb)
        pb = proj[:, E_B:E_B + LANES]
        gc128 = _chunk_cumsum(tril,g128, N_EXP)
        gce64 = _dot_x01(gc128, e64, N_EXP)
        gce128 = _dot_x01(gc128, e128, N_EXP)
        beta = _dot_x01(_sigmoid(pb), e128, N_STAT)
        egc = jnp.exp(gce128)
        kb4 = k4 * beta
        kbg4 = kb4 * egc
        vb4 = v4 * beta
        qg4 = q4 * egc
        gq = proj[:, E_GQ:E_GQ + GLA_HEADS * GLA_DK] * GLA_DK ** -0.5
        gk = proj[:, E_GK:E_GK + GLA_HEADS * GLA_DK]
        gv = proj[:, E_GV:E_GV + MIX_W]
        gate = _mm(pb, w2p_ref[...]) + vec_ref[3:4, :]
        la = jnp.maximum(-_softplus(-gate) / GLA_TAU, GLA_LOG_GATE_MIN)
        lc = _chunk_cumsum(tril,la, N_EXP)
        gqg = gq * jnp.exp(lc)
        gkg = gk * jnp.exp(-lc)
        yield

        dmask = [_decay_mask(gce64[cs], causal, eye, ones) for cs in chunks]
        la_ = [_mm_nt(jnp.concatenate([kb4[cs], q4[cs]], axis=0), _bd(k4[cs], blk128)) for cs in chunks]
        a4 = [m[CHUNK:] * d for m, d in zip(la_, dmask)]
        t4 = _tri_inv_many([jnp.where(strict, m[:CHUNK] * d, 0.0) for m, d in zip(la_, dmask)],
                           eye, levels, blk64)
        wu = [_mm(t, jnp.concatenate([_bd(kbg4[cs], blk128), _bd(vb4[cs], blk128)], axis=1))
              for t, cs in zip(t4, chunks)]
        att = [jnp.where(causal, _mm_nt(gqg[cs], _bd(gkg[cs], blk64)), 0.0) for cs in chunks]
        o_intra = [_mm(a, _bd(gv[cs], blk128)) for a, cs in zip(att, chunks)]
        gla_in = [_mm_tn(gv[cs], gkg[cs] * jnp.exp(lc[last])) for cs, last in zip(chunks, lasts)]
        yield

        for c, (cs, last) in enumerate(zip(chunks, lasts)):
            w4, u4 = wu[c][:, :MIX_W], wu[c][:, MIX_W:]
            qg_c = qg4[cs]
            gcl = gce128[last]
            kd4 = k4[cs] * jnp.exp(gcl - gce128[cs])
            gl = jnp.exp(gcl)
            v_new, q_s = [], []
            for p in range(GDN_HEADS // 2):
                psl = slice(p * 2 * GDN_DK, (p + 1) * 2 * GDN_DK)
                state = sgdn_s[s, p]
                res = _mm(jnp.concatenate([w4[:, psl], qg_c[:, psl]], axis=0), state)
                v_new.append(u4[:, psl] - res[:CHUNK])
                q_s.append(res[CHUNK:])
                sgdn_s[s, p] = jnp.where(gdn_bd, state * gl[:, psl] + _mm_tn(kd4[:, psl], v_new[p]), 0.0)
            oo_s[s, cs, :MIX_W] = (jnp.concatenate(q_s, axis=1)
                                   + _mm(a4[c], _bd(jnp.concatenate(v_new, axis=1), blk128)))
            st = sgla_s[s]
            oo_s[s, cs, MIX_W:] = o_intra[c] + _mm_nt(gqg[cs], st)
            sgla_s[s] = jnp.where(gla_bd, st * jnp.exp(lc[last]) + gla_in[c], 0.0)
        yield

        gdn_w = vec_ref[2:3, 0:LANES]
        gla_w = vec_ref[4:5, 0:LANES]
        for h in range(GDN_HEADS + GLA_HEADS):
            sl = slice(h * LANES, (h + 1) * LANES)
            if h < GDN_HEADS:
                gt, nw = proj[:, E_Z + h * LANES:E_Z + (h + 1) * LANES], gdn_w
            else:
                hh = h - GDN_HEADS
                gt, nw = proj[:, E_GR + hh * LANES:E_GR + (hh + 1) * LANES], gla_w
            oo_s[s, :, sl] = _rms(oo_s[s, :, sl], nw) * _silu(gt)
        o_ref[s] = x + jnp.dot(oo_s[s].astype(BF16), wout_ref[...], preferred_element_type=F32)

    _run_skewed([stages(s) for s in range(nseq)])


def _pad_cols(w, width, at=0):
    out = jnp.zeros((w.shape[0], width), w.dtype)
    return out.at[:, at:at + w.shape[1]].set(w)


def _expander(nblk, width):
    return (jnp.arange(LANES)[:, None] == (jnp.arange(nblk * width) // width)[None, :]).astype(BF16)


def _mixer_call(body, h, consts, scratch, name):
    bsz, seq, _ = h.shape
    rows = min(SEQ_BLOCK, seq)
    nseq = SEQS_PER_STEP if bsz % SEQS_PER_STEP == 0 else 1

    def full(a):
        return pl.BlockSpec(a.shape, lambda b, j: (0,) * a.ndim)

    blk = pl.BlockSpec((nseq, rows, D_MODEL), lambda b, j: (b, j, 0))
    return pl.pallas_call(
        functools.partial(body, rows=rows, nseq=nseq),
        grid=(bsz // nseq, seq // rows),
        in_specs=[blk] + [full(a) for a in consts],
        out_specs=blk,
        out_shape=jax.ShapeDtypeStruct(h.shape, F32),
        scratch_shapes=[pltpu.VMEM((nseq,) + s, F32) for s in scratch(rows)],
        compiler_params=pltpu.CompilerParams(
            dimension_semantics=("parallel", "arbitrary"), vmem_limit_bytes=VMEM_LIMIT),
        name=name,
    )(h, *consts)


def _even_layer(h, nw, w_in, conv_w, a_log, dt_bias, gdn_norm_w, gla_w2, gla_b, gla_norm_w, w_out):
    o = 0
    parts = []
    for s in (1536, 512, 4, 4, 256, 256, 512, 512, 16):
        parts.append(w_in[:, o:o + s])
        o += s
    w_qkv, w_z, w_b, w_a, w_gq, w_gk, w_gv, w_gr, w_glr = parts
    w_b128 = _pad_cols(w_b, LANES).at[:, E_GLR_LANE:E_GLR_LANE + GLA_GATE_RANK].set(w_glr)
    w_pack = jnp.concatenate([_pad_cols(w_a, LANES), w_b128, w_gk, w_gq, w_qkv, w_gv, w_gr, w_z],
                             axis=1).astype(BF16)
    vec = jnp.zeros((8, 2 * LANES), F32)
    vec = vec.at[0, :GDN_HEADS].set(a_log).at[1, :GDN_HEADS].set(dt_bias)
    vec = vec.at[2, :LANES].set(gdn_norm_w).at[3, :].set(gla_b).at[4, :LANES].set(gla_norm_w)
    w2p = jnp.zeros((LANES, GLA_HEADS * GLA_DK), F32).at[E_GLR_LANE:E_GLR_LANE + GLA_GATE_RANK].set(gla_w2)
    consts = (nw.reshape(1, D_MODEL), w_pack, conv_w, vec, w2p.astype(BF16),
              _expander(GROUP, CHUNK), _expander(GDN_HEADS, GDN_DK), w_out.astype(BF16))
    scratch = lambda rows: [(CARRY_ROWS, 3 * MIX_W), (GDN_HEADS // 2, 2 * GDN_DK, 2 * GDN_DK),
                            (GLA_HEADS * GLA_DV, GLA_HEADS * GLA_DK), (rows, D_MODEL)]
    return _mixer_call(_even_kernel, h, consts, scratch, "even_mixer")


O_Z, O_XBC, O_PD, O_DT, O_END = 0, 512, 1536, 3328, 3456
PD_W = 1792
C_X = SSD_HEADS * SSD_P
C_BC = SSD_GROUPS * SSD_STATE
D_HK = RWKV_HEADS * RWKV_DK


def _odd_kernel(x_ref, nw_ref, w_ref, cw_ref, cb_ref, mu_ref, vec_ref,
                w2p_ref, a2p_ref, g2_ref, exp_ref, bones_ref, wout_ref, o_ref,
                carry_s, pdc_s, sssd_s, srw_s, yy_s, *, rows, nseq):
    nc = rows // CHUNK

    @pl.when(pl.program_id(1) == 0)
    def _():
        carry_s[...] = jnp.zeros_like(carry_s)
        pdc_s[...] = jnp.zeros_like(pdc_s)
        sssd_s[...] = jnp.zeros_like(sssd_s)
        srw_s[...] = jnp.zeros_like(srw_s)

    causal, strict, eye, levels, ones = _group_consts()
    tril = _chunk_tril(rows)
    blk64 = _lane_blocks(GROUP, CHUNK)
    expand = exp_ref[...]
    bones = bones_ref[...]
    rw_bd = (_iota2((GROUP_W, GROUP_W), 0) >> _log2(RWKV_DK)) == (_iota2((GROUP_W, GROUP_W), 1) >> _log2(RWKV_DK))
    chunks = [slice(c * CHUNK, (c + 1) * CHUNK) for c in range(nc)]
    lasts = [slice((c + 1) * CHUNK - 1, (c + 1) * CHUNK) for c in range(nc)]
    groups = [slice(g * GROUP_W, (g + 1) * GROUP_W) for g in range(RWKV_HEADS // GROUP)]
    units = [(cs, gsl) for cs in chunks for gsl in groups]

    def stages(s):
        x = x_ref[s]
        hn = _rms(x, nw_ref[...]).astype(BF16)
        proj = jnp.dot(hn, w_ref[...], preferred_element_type=F32)
        z = proj[:, O_Z:O_Z + C_X]
        xbc_raw = proj[:, O_XBC:O_XBC + C_X + 2 * C_BC]
        xbc = _silu(_causal_conv(xbc_raw, carry_s[s], cw_ref[...]) + cb_ref[...])
        carry_s[s] = xbc_raw[rows - CARRY_ROWS:, :]
        xs = xbc[:, :C_X]
        b_in = xbc[:, C_X:C_X + C_BC]
        c_in = xbc[:, C_X + C_BC:]
        dtb = vec_ref[0:1, 0:LANES]
        alog = vec_ref[1:2, 0:LANES]
        dt128 = _softplus(proj[:, O_DT:O_DT + LANES] + dtb)
        ac128 = _chunk_cumsum(tril,dt128 * -jnp.exp(alog), N_EXP)
        ace = _dot_x01(ac128, expand, N_EXP)
        xdt = xs * _dot_x01(dt128, expand, N_STAT)
        e_ac = jnp.exp(ace)
        pd = proj[:, O_PD:O_PD + PD_W]
        pd = pd + (_shifted_rows(pd, pdc_s[s], 1) - pd) * mu_ref[...]
        pdc_s[s] = proj[rows - CARRY_ROWS:, O_PD:O_PD + PD_W]
        r = pd[:, 0:D_HK]
        k = pd[:, D_HK:2 * D_HK]
        v = pd[:, 2 * D_HK:3 * D_HK]
        xwa = pd[:, 3 * D_HK:3 * D_HK + LANES]
        xg = pd[:, 3 * D_HK + LANES:]
        w0, a0 = vec_ref[2:3, :], vec_ref[3:4, :]
        k_k, k_a, r_k = vec_ref[4:5, :], vec_ref[5:6, :], vec_ref[6:7, :]
        w_log = -_softplus(-(w0 + _mm(jnp.tanh(xwa), w2p_ref[...]))) - 0.5
        lw = -jnp.exp(w_log)
        a = _sigmoid(a0 + _mm(xwa, a2p_ref[...]))
        g_out = _mm(_sigmoid(xg), g2_ref[...])
        kkr = k * k_k
        kk = kkr * lax.rsqrt(_dot_x01(kkr * kkr, bones, N_STAT) + L2_EPS)
        k2 = k * (1.0 + (a - 1.0) * k_a)
        be = kk * a
        pc = _chunk_cumsum(tril,lw, N_EXP)
        e_np = jnp.exp(-pc)
        a_hat = -kk * jnp.exp(pc - lw)
        b_hat = be * e_np
        k_hat = k2 * e_np
        r_hat = r * jnp.exp(pc)
        yield

        cbseg = []
        for cs in chunks:
            for g, gsl in enumerate(groups):
                ssl = slice(g * SSD_STATE, (g + 1) * SSD_STATE)
                cb4 = _mm_nt(c_in[cs, ssl], jnp.concatenate([b_in[cs, ssl].astype(BF16)] * GROUP, axis=0))
                cbseg.append(cb4 * _decay_mask(ace[cs, gsl], causal, eye, ones))
        y_intra = [_mm(m, _bd(xdt[cs, gsl], blk64)) for m, (cs, gsl) in zip(cbseg, units)]
        ar = [jnp.concatenate([a_hat[cs, gsl], r_hat[cs, gsl]], axis=0) for cs, gsl in units]
        mb = [_mm_nt(m, _bd(b_hat[cs, gsl], blk64)) for m, (cs, gsl) in zip(ar, units)]
        mk = [_mm_nt(m, _bd(k_hat[cs, gsl], blk64)) for m, (cs, gsl) in zip(ar, units)]
        m_rb = [jnp.where(causal, m[CHUNK:], 0.0) for m in mb]
        t4 = _tri_inv_many([jnp.where(strict, -m[:CHUNK], 0.0) for m in mb], eye, levels, blk64)
        mv = [_mm(jnp.concatenate([jnp.where(strict, m[:CHUNK], 0.0), jnp.where(causal, m[CHUNK:], 0.0)], axis=0),
                  _bd(v[cs, gsl], blk64)) for m, (cs, gsl) in zip(mk, units)]
        tu = [_mm(t, jnp.concatenate([_bd(a_hat[cs, gsl], blk64), _bd(m[:CHUNK], blk64)], axis=1))
              for t, m, (cs, gsl) in zip(t4, mv, units)]
        yield

        for c, (cs, last) in enumerate(zip(chunks, lasts)):
            acl = ace[last]
            e_dec = jnp.exp(acl - ace[cs])
            cd = jnp.exp(acl)
            p_c = jnp.exp(pc[last])
            b_til = b_hat[cs] * p_c
            k_til = k_hat[cs] * p_c
            for g, gsl in enumerate(groups):
                i = c * len(groups) + g
                ssl = slice(g * SSD_STATE, (g + 1) * SSD_STATE)
                state = sssd_s[s, g]
                yy_s[s, cs, gsl] = y_intra[i] + _mm(c_in[cs, ssl], state) * e_ac[cs, gsl]
                sssd_s[s, g] = state * cd[:, gsl] + _mm_tn(b_in[cs, ssl], xdt[cs, gsl] * e_dec[:, gsl])
                a_til, u0 = tu[i][:, :GROUP_W], tu[i][:, GROUP_W:]
                state = srw_s[s, g]
                res = _mm_nt(jnp.concatenate([a_til, r_hat[cs, gsl]], axis=0), state)
                u = u0 + res[:CHUNK]
                yy_s[s, cs, C_X + g * GROUP_W:C_X + (g + 1) * GROUP_W] = (
                    mv[i][CHUNK:] + res[CHUNK:] + _mm(m_rb[i], _bd(u, blk64)))
                new = state * p_c[:, gsl] + _mm_tn(jnp.concatenate([u, v[cs, gsl]], axis=0),
                                                   jnp.concatenate([b_til[:, gsl], k_til[:, gsl]], axis=0))
                srw_s[s, g] = jnp.where(rw_bd, new, 0.0)
        yield

        d_skip = vec_ref[7:8, :]
        yc = (yy_s[s, :, 0:C_X] + xs * d_skip) * _silu(z)
        gw = C_X // SSD_GROUPS
        for g in range(SSD_GROUPS):
            gsl = slice(g * gw, (g + 1) * gw)
            yy_s[s, :, gsl] = _rms(yc[:, gsl], vec_ref[8:9, gsl])
        yd = yy_s[s, :, C_X:]
        inv_n = 1.0 / RWKV_DK
        mean = _dot_x01(yd, bones, N_STAT) * inv_n
        cen = yd - mean
        var = _dot_x01(cen * cen, bones, N_STAT) * inv_n
        yd = cen * lax.rsqrt(var + RWKV_GN_EPS) * vec_ref[9:10, :] + vec_ref[10:11, :]
        bonus = _dot_x01(r * k2 * r_k, bones, N_STAT)
        yy_s[s, :, C_X:] = (yd + bonus * v) * g_out
        o_ref[s] = x + jnp.dot(yy_s[s].astype(BF16), wout_ref[...], preferred_element_type=F32)

    _run_skewed([stages(s) for s in range(nseq)])


def _odd_layer(h, nw, w_in, conv_w, conv_b, dt_bias, a_log, d_skip, ssd_norm_w, mu, w0, w2, a0, a2, g2,
               k_k, k_a, r_k, gn_w, gn_b, w_out):
    w_z = w_in[:, 0:512]
    w_xbc = w_in[:, 512:1536]
    w_dt = w_in[:, 1536:1544]
    w_pd = w_in[:, 1544:1544 + PD_W]
    w_pack = jnp.concatenate([w_z, w_xbc, w_pd, _pad_cols(w_dt, LANES)], axis=1).astype(BF16)
    vec = jnp.zeros((16, D_HK), F32)
    vec = vec.at[0, :SSD_HEADS].set(dt_bias).at[1, :SSD_HEADS].set(a_log)
    vec = vec.at[2].set(w0).at[3].set(a0).at[4].set(k_k).at[5].set(k_a).at[6].set(r_k.reshape(-1))
    vec = vec.at[7].set(jnp.repeat(d_skip, SSD_P)).at[8].set(ssd_norm_w).at[9].set(gn_w).at[10].set(gn_b)
    rank = w2.shape[0]
    w2p = jnp.zeros((LANES, D_HK), F32).at[:rank].set(w2).astype(BF16)
    a2p = jnp.zeros((LANES, D_HK), F32).at[rank:rank + a2.shape[0]].set(a2).astype(BF16)
    head_of_lane = jnp.arange(D_HK) // RWKV_DK
    bones = (head_of_lane[:, None] == head_of_lane[None, :]).astype(BF16)
    consts = (nw.reshape(1, D_MODEL), w_pack, conv_w, conv_b.reshape(1, -1), mu.reshape(1, PD_W),
              vec, w2p, a2p, g2.astype(BF16), _expander(SSD_HEADS, SSD_P), bones, w_out.astype(BF16))
    scratch = lambda rows: [(CARRY_ROWS, C_X + 2 * C_BC), (CARRY_ROWS, PD_W),
                            (SSD_GROUPS, SSD_STATE, GROUP_W), (RWKV_HEADS // GROUP, GROUP_W, GROUP_W),
                            (rows, D_MODEL)]
    return _mixer_call(_odd_kernel, h, consts, scratch, "odd_mixer")


def kernel(x, norm_mix_w, norm_mlp_w, mlp_w1, mlp_w2, final_norm_w, even_w_in, gdn_conv_w, gdn_a_log, gdn_dt_bias, gdn_norm_w, gla_gate_w2, gla_gate_b, gla_norm_w, even_w_out, odd_w_in, ssd_conv_w, ssd_conv_b, ssd_dt_bias, ssd_a_log, ssd_d, ssd_norm_w, rwkv_mu, rwkv_w0, rwkv_w2, rwkv_a0, rwkv_a2, rwkv_g2, rwkv_k_k, rwkv_k_a, rwkv_r_k, rwkv_gn_w, rwkv_gn_b, odd_w_out):
    bsz, seq, _ = x.shape
    h = x
    for layer in range(DEPTH):
        i = layer // 2
        if layer % 2 == 0:
            h = _even_layer(h, norm_mix_w[layer], even_w_in[i], gdn_conv_w[i], gdn_a_log[i], gdn_dt_bias[i],
                            gdn_norm_w[i], gla_gate_w2[i], gla_gate_b[i], gla_norm_w[i], even_w_out[i])
        else:
            h = _odd_layer(h, norm_mix_w[layer], odd_w_in[i], ssd_conv_w[i], ssd_conv_b[i], ssd_dt_bias[i],
                           ssd_a_log[i], ssd_d[i], ssd_norm_w[i], rwkv_mu[i], rwkv_w0[i], rwkv_w2[i],
                           rwkv_a0[i], rwkv_a2[i], rwkv_g2[i], rwkv_k_k[i], rwkv_k_a[i], rwkv_r_k[i],
                           rwkv_gn_w[i], rwkv_gn_b[i], odd_w_out[i])
        last = layer == DEPTH - 1
        h = _mlp_layer(h.reshape(bsz * seq, D_MODEL), norm_mlp_w[layer], mlp_w1[layer], mlp_w2[layer],
                       final_norm_w, last).reshape(bsz, seq, D_MODEL)
    return h
```

```python
import functools

import jax
import jax.numpy as jnp
from jax import lax
from jax.experimental import pallas as pl
from jax.experimental.pallas import tpu as pltpu

F32 = jnp.float32
BF16 = jnp.bfloat16

D_MODEL = 1024
DEPTH = 4
CHUNK = 64
CONV_W = 4
D_FF = 4 * D_MODEL
NORM_EPS = 1e-6
L2_EPS = 1e-6
MIX_W = D_MODEL // 2
GDN_HEADS = 4
GDN_DK = 128
GLA_HEADS = 4
GLA_DK = 64
GLA_DV = 128
GLA_GATE_RANK = 16
GLA_TAU = 16.0
GLA_LOG_GATE_MIN = -1.0
SSD_HEADS = 8
SSD_P = 64
SSD_GROUPS = 2
SSD_STATE = 128
RWKV_HEADS = 8
RWKV_DK = 64
RWKV_GN_EPS = 64e-5

LANES = 128
CARRY_ROWS = 8
GROUP = 4
GROUP_W = GROUP * CHUNK
SEQ_BLOCK = 512
SEQS_PER_STEP = 1
N_EXP = 2
N_STAT = 1
MLP_ROWS = 1024
MLP_FF_BLOCK = 1024
VMEM_LIMIT = 56 * 1024 * 1024


def _mm(a, b):
    return jnp.dot(a.astype(BF16), b.astype(BF16), preferred_element_type=F32)


def _mm_nt(a, b):
    return lax.dot_general(a.astype(BF16), b.astype(BF16), (((1,), (1,)), ((), ())),
                           preferred_element_type=F32)


def _mm_tn(a, b):
    return lax.dot_general(a.astype(BF16), b.astype(BF16), (((0,), (0,)), ((), ())),
                           preferred_element_type=F32)


def _split(x, n):
    pieces = []
    rem = x
    for i in range(n):
        p = rem.astype(BF16)
        pieces.append(p)
        if i + 1 < n:
            rem = rem - p.astype(F32)
    return pieces


def _dot_x01(x, m01, n):
    return sum(jnp.dot(p, m01, preferred_element_type=F32) for p in _split(x, n))


def _dot_01x(m01, x, n):
    return sum(jnp.dot(m01, p, preferred_element_type=F32) for p in _split(x, n))


def _rms(x, w, eps=NORM_EPS):
    return x * lax.rsqrt(jnp.mean(x * x, axis=-1, keepdims=True) + eps) * w


def _softplus(x):
    return jnp.maximum(x, 0.0) + jnp.log1p(jnp.exp(-jnp.abs(x)))


def _sigmoid(x):
    return 0.5 * jnp.tanh(0.5 * x) + 0.5


def _silu(x):
    return x * _sigmoid(x)


def _iota2(shape, dim):
    return lax.broadcasted_iota(jnp.int32, shape, dim)


def _log2(n):
    assert n & (n - 1) == 0
    return n.bit_length() - 1


def _lane_blocks(nblk, width):
    lane = _iota2((1, nblk * width), 1)
    return [(lane >= h * width) & (lane < (h + 1) * width) for h in range(nblk)]


def _bd(x, masks):
    return jnp.concatenate([jnp.where(m, x, 0.0).astype(BF16) for m in masks], axis=0)


def _group_consts():
    ii = _iota2((CHUNK, GROUP_W), 0)
    jj = _iota2((CHUNK, GROUP_W), 1) & (CHUNK - 1)
    causal = ii >= jj
    strict = ii > jj
    eye = jnp.where(ii == jj, 1.0, 0.0).astype(F32)
    levels = []
    k = 0
    while (1 << k) < CHUNK:
        same = (ii >> (k + 1)) == (jj >> (k + 1))
        lower = ((ii >> k) & 1) == 1
        left = ((jj >> k) & 1) == 0
        levels.append(jnp.where(same & lower & left, 1.0, 0.0).astype(F32))
        k += 1
    ones = jnp.ones((CHUNK, CHUNK), BF16)
    return causal, strict, eye, levels, ones


def _chunk_tril(rows):
    n = min(rows, GROUP_W)
    ii = _iota2((n, n), 0)
    jj = _iota2((n, n), 1)
    return jnp.where(((ii >> _log2(CHUNK)) == (jj >> _log2(CHUNK))) & (ii >= jj), 1.0, 0.0).astype(BF16)


def _chunk_cumsum(tril, x, n):
    t = tril.shape[0]
    return jnp.concatenate([_dot_01x(tril, x[i:i + t], n) for i in range(0, x.shape[0], t)], axis=0)


def _tri_inv_many(ls, eye, levels, blocks):
    xs = [eye - l * levels[0] for l in ls]
    for lvl in levels[1:]:
        ys = [_mm(x, _bd(l * lvl, blocks)) for x, l in zip(xs, ls)]
        xs = [x - _mm(y, _bd(x, blocks)) for x, y in zip(xs, ys)]
    return xs


def _rowcast(colx, eye, ones):
    return _dot_01x(ones, colx * eye, N_EXP)


def _decay_mask(colx, mask, eye, ones):
    return jnp.where(mask, jnp.exp(jnp.where(mask, colx - _rowcast(colx, eye, ones), 0.0)), 0.0)


def _shifted_rows(cur, carry, shift):
    full = jnp.concatenate([carry, cur], axis=0)
    if shift:
        full = pltpu.roll(full, shift, 0)
    return full[CARRY_ROWS:]


def _causal_conv(cur, carry, w):
    out = None
    for j in range(CONV_W):
        term = _shifted_rows(cur, carry, CONV_W - 1 - j) * w[j:j + 1, :]
        out = term if out is None else out + term
    return out


def _run_skewed(stage_gens):
    live = [True] * len(stage_gens)
    t = 0
    while any(live):
        for i, g in enumerate(stage_gens):
            if live[i] and t >= i:
                try:
                    next(g)
                except StopIteration:
                    live[i] = False
        t += 1


def _mlp_kernel(x_ref, nw_ref, w1_ref, w2_ref, fw_ref, o_ref, *, final_norm):
    x = x_ref[...]
    hn = _rms(x, nw_ref[...]).astype(BF16)
    acc = x
    for c in range(D_FF // MLP_FF_BLOCK):
        cols = slice(c * MLP_FF_BLOCK, (c + 1) * MLP_FF_BLOCK)
        h = jnp.dot(hn, w1_ref[:, cols], preferred_element_type=F32)
        h = jnp.square(jnp.maximum(h, 0.0)).astype(BF16)
        acc = acc + jnp.dot(h, w2_ref[cols, :], preferred_element_type=F32)
    if final_norm:
        acc = _rms(acc, fw_ref[...])
    o_ref[...] = acc


def _mlp_layer(h2d, nw, w1, w2, fw, final_norm):
    t = h2d.shape[0]
    tm = min(MLP_ROWS, t)
    const = lambda i: (0, 0)
    return pl.pallas_call(
        functools.partial(_mlp_kernel, final_norm=final_norm),
        grid=(t // tm,),
        in_specs=[
            pl.BlockSpec((tm, D_MODEL), lambda i: (i, 0)),
            pl.BlockSpec((1, D_MODEL), const),
            pl.BlockSpec((D_MODEL, D_FF), const),
            pl.BlockSpec((D_FF, D_MODEL), const),
            pl.BlockSpec((1, D_MODEL), const),
        ],
        out_specs=pl.BlockSpec((tm, D_MODEL), lambda i: (i, 0)),
        out_shape=jax.ShapeDtypeStruct((t, D_MODEL), F32),
        compiler_params=pltpu.CompilerParams(
            dimension_semantics=("parallel",), vmem_limit_bytes=VMEM_LIMIT),
        name="mlp_final" if final_norm else "mlp",
    )(h2d, nw.reshape(1, D_MODEL), w1.astype(BF16), w2.astype(BF16), fw.reshape(1, D_MODEL))


E_A, E_B, E_GK, E_GQ, E_QKV, E_GV, E_GR, E_Z, E_END = 0, 128, 256, 512, 768, 2304, 2816, 3328, 3840
E_GLR_LANE = 8


def _even_kernel(x_ref, nw_ref, w_ref, cw_ref, vec_ref, w2p_ref, e64_ref, e128_ref, wout_ref, o_ref,
                 carry_s, sgdn_s, sgla_s, oo_s, *, rows, nseq):
    nc = rows // CHUNK

    @pl.when(pl.program_id(1) == 0)
    def _():
        carry_s[...] = jnp.zeros_like(carry_s)
        sgdn_s[...] = jnp.zeros_like(sgdn_s)
        sgla_s[...] = jnp.zeros_like(sgla_s)

    causal, strict, eye, levels, ones = _group_consts()
    tril = _chunk_tril(rows)
    blk64 = _lane_blocks(GROUP, CHUNK)
    blk128 = _lane_blocks(GDN_HEADS, GDN_DK)
    e64, e128 = e64_ref[...], e128_ref[...]
    pair_r = _iota2((2 * GDN_DK, 2 * GDN_DK), 0) >= GDN_DK
    pair_c = _iota2((2 * GDN_DK, 2 * GDN_DK), 1) >= GDN_DK
    gdn_bd = pair_r == pair_c
    gla_bd = (_iota2((GLA_HEADS * GLA_DV, GLA_HEADS * GLA_DK), 0) >> _log2(GLA_DV)) == \
             (_iota2((GLA_HEADS * GLA_DV, GLA_HEADS * GLA_DK), 1) >> _log2(GLA_DK))
    chunks = [slice(c * CHUNK, (c + 1) * CHUNK) for c in range(nc)]
    lasts = [slice((c + 1) * CHUNK - 1, (c + 1) * CHUNK) for c in range(nc)]

    def stages(s):
        x = x_ref[s]
        hn = _rms(x, nw_ref[...]).astype(BF16)
        proj = jnp.dot(hn, w_ref[...], preferred_element_type=F32)
        qkv_raw = proj[:, E_QKV:E_QKV + 3 * MIX_W]
        qkv = _silu(_causal_conv(qkv_raw, carry_s[s], cw_ref[...]))
        carry_s[s] = qkv_raw[rows - CARRY_ROWS:, :]
        qs, ks = [], []
        for h in range(GDN_HEADS):
            qh = qkv[:, h * GDN_DK:(h + 1) * GDN_DK]
            kh = qkv[:, MIX_W + h * GDN_DK:MIX_W + (h + 1) * GDN_DK]
            qs.append(qh * (lax.rsqrt(jnp.sum(qh * qh, axis=-1, keepdims=True) + L2_EPS) * GDN_DK ** -0.5))
            ks.append(kh * lax.rsqrt(jnp.sum(kh * kh, axis=-1, keepdims=True) + L2_EPS))
        q4 = jnp.concatenate(qs, axis=1)
        k4 = jnp.concatenate(ks, axis=1)
        v4 = qkv[:, 2 * MIX_W:]
        alog = vec_ref[0:1, 0:LANES]
        dtb = vec_ref[1:2, 0:LANES]
        g128 = -jnp.exp(alog) * _softplus(proj[:, E_A:E_A + LANES] + dtb)
        pb = proj[:, E_B:E_B + LANES]
        gc128 = _chunk_cumsum(tril,g128, N_EXP)
        gce64 = _dot_x01(gc128, e64, N_EXP)
        gce128 = _dot_x01(gc128, e128, N_EXP)
        beta = _dot_x01(_sigmoid(pb), e128, N_STAT)
        egc = jnp.exp(gce128)
        kb4 = k4 * beta
        kbg4 = kb4 * egc
        vb4 = v4 * beta
        qg4 = q4 * egc
        gq = proj[:, E_GQ:E_GQ + GLA_HEADS * GLA_DK] * GLA_DK ** -0.5
        gk = proj[:, E_GK:E_GK + GLA_HEADS * GLA_DK]
        gv = proj[:, E_GV:E_GV + MIX_W]
        gate = _mm(pb, w2p_ref[...]) + vec_ref[3:4, :]
        la = jnp.maximum(-_softplus(-gate) / GLA_TAU, GLA_LOG_GATE_MIN)
        lc = _chunk_cumsum(tril,la, N_EXP)
        gqg = gq * jnp.exp(lc)
        gkg = gk * jnp.exp(-lc)
        yield

        dmask = [_decay_mask(gce64[cs], causal, eye, ones) for cs in chunks]
        la_ = [_mm_nt(jnp.concatenate([kb4[cs], q4[cs]], axis=0), _bd(k4[cs], blk128)) for cs in chunks]
        a4 = [m[CHUNK:] * d for m, d in zip(la_, dmask)]
        t4 = _tri_inv_many([jnp.where(strict, m[:CHUNK] * d, 0.0) for m, d in zip(la_, dmask)],
                           eye, levels, blk64)
        wu = [_mm(t, jnp.concatenate([_bd(kbg4[cs], blk128), _bd(vb4[cs], blk128)], axis=1))
              for t, cs in zip(t4, chunks)]
        att = [jnp.where(causal, _mm_nt(gqg[cs], _bd(gkg[cs], blk64)), 0.0) for cs in chunks]
        o_intra = [_mm(a, _bd(gv[cs], blk128)) for a, cs in zip(att, chunks)]
        gla_in = [_mm_tn(gv[cs], gkg[cs] * jnp.exp(lc[last])) for cs, last in zip(chunks, lasts)]
        yield

        for c, (cs, last) in enumerate(zip(chunks, lasts)):
            w4, u4 = wu[c][:, :MIX_W], wu[c][:, MIX_W:]
            qg_c = qg4[cs]
            gcl = gce128[last]
            kd4 = k4[cs] * jnp.exp(gcl - gce128[cs])
            gl = jnp.exp(gcl)
            v_new, q_s = [], []
            for p in range(GDN_HEADS // 2):
                psl = slice(p * 2 * GDN_DK, (p + 1) * 2 * GDN_DK)
                state = sgdn_s[s, p]
                res = _mm(jnp.concatenate([w4[:, psl], qg_c[:, psl]], axis=0), state)
                v_new.append(u4[:, psl] - res[:CHUNK])
                q_s.append(res[CHUNK:])
                sgdn_s[s, p] = jnp.where(gdn_bd, state * gl[:, psl] + _mm_tn(kd4[:, psl], v_new[p]), 0.0)
            oo_s[s, cs, :MIX_W] = (jnp.concatenate(q_s, axis=1)
                                   + _mm(a4[c], _bd(jnp.concatenate(v_new, axis=1), blk128)))
            st = sgla_s[s]
            oo_s[s, cs, MIX_W:] = o_intra[c] + _mm_nt(gqg[cs], st)
            sgla_s[s] = jnp.where(gla_bd, st * jnp.exp(lc[last]) + gla_in[c], 0.0)
        yield

        gdn_w = vec_ref[2:3, 0:LANES]
        gla_w = vec_ref[4:5, 0:LANES]
        for h in range(GDN_HEADS + GLA_HEADS):
            sl = slice(h * LANES, (h + 1) * LANES)
            if h < GDN_HEADS:
                gt, nw = proj[:, E_Z + h * LANES:E_Z + (h + 1) * LANES], gdn_w
            else:
                hh = h - GDN_HEADS
                gt, nw = proj[:, E_GR + hh * LANES:E_GR + (hh + 1) * LANES], gla_w
            oo_s[s, :, sl] = _rms(oo_s[s, :, sl], nw) * _silu(gt)
        o_ref[s] = x + jnp.dot(oo_s[s].astype(BF16), wout_ref[...], preferred_element_type=F32)

    _run_skewed([stages(s) for s in range(nseq)])


def _pad_cols(w, width, at=0):
    out = jnp.zeros((w.shape[0], width), w.dtype)
    return out.at[:, at:at + w.shape[1]].set(w)


def _expander(nblk, width):
    return (jnp.arange(LANES)[:, None] == (jnp.arange(nblk * width) // width)[None, :]).astype(BF16)


def _mixer_call(body, h, consts, scratch, name):
    bsz, seq, _ = h.shape
    rows = min(SEQ_BLOCK, seq)
    nseq = SEQS_PER_STEP if bsz % SEQS_PER_STEP == 0 else 1

    def full(a):
        return pl.BlockSpec(a.shape, lambda b, j: (0,) * a.ndim)

    blk = pl.BlockSpec((nseq, rows, D_MODEL), lambda b, j: (b, j, 0))
    return pl.pallas_call(
        functools.partial(body, rows=rows, nseq=nseq),
        grid=(bsz // nseq, seq // rows),
        in_specs=[blk] + [full(a) for a in consts],
        out_specs=blk,
        out_shape=jax.ShapeDtypeStruct(h.shape, F32),
        scratch_shapes=[pltpu.VMEM((nseq,) + s, F32) for s in scratch(rows)],
        compiler_params=pltpu.CompilerParams(
            dimension_semantics=("parallel", "arbitrary"), vmem_limit_bytes=VMEM_LIMIT),
        name=name,
    )(h, *consts)


def _even_layer(h, nw, w_in, conv_w, a_log, dt_bias, gdn_norm_w, gla_w2, gla_b, gla_norm_w, w_out):
    o = 0
    parts = []
    for s in (1536, 512, 4, 4, 256, 256, 512, 512, 16):
        parts.append(w_in[:, o:o + s])
        o += s
    w_qkv, w_z, w_b, w_a, w_gq, w_gk, w_gv, w_gr, w_glr = parts
    w_b128 = _pad_cols(w_b, LANES).at[:, E_GLR_LANE:E_GLR_LANE + GLA_GATE_RANK].set(w_glr)
    w_pack = jnp.concatenate([_pad_cols(w_a, LANES), w_b128, w_gk, w_gq, w_qkv, w_gv, w_gr, w_z],
                             axis=1).astype(BF16)
    vec = jnp.zeros((8, 2 * LANES), F32)
    vec = vec.at[0, :GDN_HEADS].set(a_log).at[1, :GDN_HEADS].set(dt_bias)
    vec = vec.at[2, :LANES].set(gdn_norm_w).at[3, :].set(gla_b).at[4, :LANES].set(gla_norm_w)
    w2p = jnp.zeros((LANES, GLA_HEADS * GLA_DK), F32).at[E_GLR_LANE:E_GLR_LANE + GLA_GATE_RANK].set(gla_w2)
    consts = (nw.reshape(1, D_MODEL), w_pack, conv_w, vec, w2p.astype(BF16),
              _expander(GROUP, CHUNK), _expander(GDN_HEADS, GDN_DK), w_out.astype(BF16))
    scratch = lambda rows: [(CARRY_ROWS, 3 * MIX_W), (GDN_HEADS // 2, 2 * GDN_DK, 2 * GDN_DK),
                            (GLA_HEADS * GLA_DV, GLA_HEADS * GLA_DK), (rows, D_MODEL)]
    return _mixer_call(_even_kernel, h, consts, scratch, "even_mixer")


O_Z, O_XBC, O_PD, O_DT, O_END = 0, 512, 1536, 3328, 3456
PD_W = 1792
C_X = SSD_HEADS * SSD_P
C_BC = SSD_GROUPS * SSD_STATE
D_HK = RWKV_HEADS * RWKV_DK


def _odd_kernel(x_ref, nw_ref, w_ref, cw_ref, cb_ref, mu_ref, vec_ref,
                w2p_ref, a2p_ref, g2_ref, exp_ref, bones_ref, wout_ref, o_ref,
                carry_s, pdc_s, sssd_s, srw_s, yy_s, *, rows, nseq):
    nc = rows // CHUNK

    @pl.when(pl.program_id(1) == 0)
    def _():
        carry_s[...] = jnp.zeros_like(carry_s)
        pdc_s[...] = jnp.zeros_like(pdc_s)
        sssd_s[...] = jnp.zeros_like(sssd_s)
        srw_s[...] = jnp.zeros_like(srw_s)

    causal, strict, eye, levels, ones = _group_consts()
    tril = _chunk_tril(rows)
    blk64 = _lane_blocks(GROUP, CHUNK)
    expand = exp_ref[...]
    bones = bones_ref[...]
    rw_bd = (_iota2((GROUP_W, GROUP_W), 0) >> _log2(RWKV_DK)) == (_iota2((GROUP_W, GROUP_W), 1) >> _log2(RWKV_DK))
    chunks = [slice(c * CHUNK, (c + 1) * CHUNK) for c in range(nc)]
    lasts = [slice((c + 1) * CHUNK - 1, (c + 1) * CHUNK) for c in range(nc)]
    groups = [slice(g * GROUP_W, (g + 1) * GROUP_W) for g in range(RWKV_HEADS // GROUP)]
    units = [(cs, gsl) for cs in chunks for gsl in groups]

    def stages(s):
        x = x_ref[s]
        hn = _rms(x, nw_ref[...]).astype(BF16)
        proj = jnp.dot(hn, w_ref[...], preferred_element_type=F32)
        z = proj[:, O_Z:O_Z + C_X]
        xbc_raw = proj[:, O_XBC:O_XBC + C_X + 2 * C_BC]
        xbc = _silu(_causal_conv(xbc_raw, carry_s[s], cw_ref[...]) + cb_ref[...])
        carry_s[s] = xbc_raw[rows - CARRY_ROWS:, :]
        xs = xbc[:, :C_X]
        b_in = xbc[:, C_X:C_X + C_BC]
        c_in = xbc[:, C_X + C_BC:]
        dtb = vec_ref[0:1, 0:LANES]
        alog = vec_ref[1:2, 0:LANES]
        dt128 = _softplus(proj[:, O_DT:O_DT + LANES] + dtb)
        ac128 = _chunk_cumsum(tril,dt128 * -jnp.exp(alog), N_EXP)
        ace = _dot_x01(ac128, expand, N_EXP)
        xdt = xs * _dot_x01(dt128, expand, N_STAT)
        e_ac = jnp.exp(ace)
        pd = proj[:, O_PD:O_PD + PD_W]
        pd = pd + (_shifted_rows(pd, pdc_s[s], 1) - pd) * mu_ref[...]
        pdc_s[s] = proj[rows - CARRY_ROWS:, O_PD:O_PD + PD_W]
        r = pd[:, 0:D_HK]
        k = pd[:, D_HK:2 * D_HK]
        v = pd[:, 2 * D_HK:3 * D_HK]
        xwa = pd[:, 3 * D_HK:3 * D_HK + LANES]
        xg = pd[:, 3 * D_HK + LANES:]
        w0, a0 = vec_ref[2:3, :], vec_ref[3:4, :]
        k_k, k_a, r_k = vec_ref[4:5, :], vec_ref[5:6, :], vec_ref[6:7, :]
        w_log = -_softplus(-(w0 + _mm(jnp.tanh(xwa), w2p_ref[...]))) - 0.5
        lw = -jnp.exp(w_log)
        a = _sigmoid(a0 + _mm(xwa, a2p_ref[...]))
        g_out = _mm(_sigmoid(xg), g2_ref[...])
        kkr = k * k_k
        kk = kkr * lax.rsqrt(_dot_x01(kkr * kkr, bones, N_STAT) + L2_EPS)
        k2 = k * (1.0 + (a - 1.0) * k_a)
        be = kk * a
        pc = _chunk_cumsum(tril,lw, N_EXP)
        e_np = jnp.exp(-pc)
        a_hat = -kk * jnp.exp(pc - lw)
        b_hat = be * e_np
        k_hat = k2 * e_np
        r_hat = r * jnp.exp(pc)
        yield

        cbseg = []
        for cs in chunks:
            for g, gsl in enumerate(groups):
                ssl = slice(g * SSD_STATE, (g + 1) * SSD_STATE)
                cb4 = _mm_nt(c_in[cs, ssl], jnp.concatenate([b_in[cs, ssl].astype(BF16)] * GROUP, axis=0))
                cbseg.append(cb4 * _decay_mask(ace[cs, gsl], causal, eye, ones))
        y_intra = [_mm(m, _bd(xdt[cs, gsl], blk64)) for m, (cs, gsl) in zip(cbseg, units)]
        ar = [jnp.concatenate([a_hat[cs, gsl], r_hat[cs, gsl]], axis=0) for cs, gsl in units]
        mb = [_mm_nt(m, _bd(b_hat[cs, gsl], blk64)) for m, (cs, gsl) in zip(ar, units)]
        mk = [_mm_nt(m, _bd(k_hat[cs, gsl], blk64)) for m, (cs, gsl) in zip(ar, units)]
        m_rb = [jnp.where(causal, m[CHUNK:], 0.0) for m in mb]
        t4 = _tri_inv_many([jnp.where(strict, -m[:CHUNK], 0.0) for m in mb], eye, levels, blk64)
        mv = [_mm(jnp.concatenate([jnp.where(strict, m[:CHUNK], 0.0), jnp.where(causal, m[CHUNK:], 0.0)], axis=0),
                  _bd(v[cs, gsl], blk64)) for m, (cs, gsl) in zip(mk, units)]
        tu = [_mm(t, jnp.concatenate([_bd(a_hat[cs, gsl], blk64), _bd(m[:CHUNK], blk64)], axis=1))
              for t, m, (cs, gsl) in zip(t4, mv, units)]
        p_c = [jnp.exp(pc[last]) for last in lasts]
        cd = [jnp.exp(ace[last]) for last in lasts]
        rw_mat, rw_add, ssd_add = [], [], []
        for c, (cs, last) in enumerate(zip(chunks, lasts)):
            e_dec = jnp.exp(ace[last] - ace[cs])
            for g, gsl in enumerate(groups):
                i = c * len(groups) + g
                ssl = slice(g * SSD_STATE, (g + 1) * SSD_STATE)
                b_til = b_hat[cs, gsl] * p_c[c][:, gsl]
                k_til = k_hat[cs, gsl] * p_c[c][:, gsl]
                rw_mat.append(_mm_tn(tu[i][:, :GROUP_W], b_til))
                rw_add.append(_mm_tn(jnp.concatenate([tu[i][:, GROUP_W:], v[cs, gsl]], axis=0),
                                     jnp.concatenate([b_til, k_til], axis=0)))
                ssd_add.append(_mm_tn(b_in[cs, ssl], xdt[cs, gsl] * e_dec[:, gsl]))
        yield

        rw_state = [srw_s[s, g] for g in range(len(groups))]
        ssd_state = [sssd_s[s, g] for g in range(len(groups))]
        for c, cs in enumerate(chunks):
            start = [(rw_state[g], ssd_state[g]) for g in range(len(groups))]
            for g, gsl in enumerate(groups):
                i = c * len(groups) + g
                st = rw_state[g]
                rw_state[g] = jnp.where(rw_bd, st * p_c[c][:, gsl] + _mm(st, rw_mat[i]) + rw_add[i], 0.0)
                ssd_state[g] = ssd_state[g] * cd[c][:, gsl] + ssd_add[i]
            for g, gsl in enumerate(groups):
                i = c * len(groups) + g
                st_rw, st_ssd = start[g]
                ssl = slice(g * SSD_STATE, (g + 1) * SSD_STATE)
                yy_s[s, cs, gsl] = y_intra[i] + _mm(c_in[cs, ssl], st_ssd) * e_ac[cs, gsl]
                res = _mm_nt(jnp.concatenate([tu[i][:, :GROUP_W], r_hat[cs, gsl]], axis=0), st_rw)
                u = tu[i][:, GROUP_W:] + res[:CHUNK]
                yy_s[s, cs, C_X + g * GROUP_W:C_X + (g + 1) * GROUP_W] = (
                    mv[i][CHUNK:] + res[CHUNK:] + _mm(m_rb[i], _bd(u, blk64)))
        for g in range(len(groups)):
            srw_s[s, g] = rw_state[g]
            sssd_s[s, g] = ssd_state[g]
        yield

        d_skip = vec_ref[7:8, :]
        yc = (yy_s[s, :, 0:C_X] + xs * d_skip) * _silu(z)
        gw = C_X // SSD_GROUPS
        for g in range(SSD_GROUPS):
            gsl = slice(g * gw, (g + 1) * gw)
            yy_s[s, :, gsl] = _rms(yc[:, gsl], vec_ref[8:9, gsl])
        yd = yy_s[s, :, C_X:]
        inv_n = 1.0 / RWKV_DK
        mean = _dot_x01(yd, bones, N_STAT) * inv_n
        cen = yd - mean
        var = _dot_x01(cen * cen, bones, N_STAT) * inv_n
        yd = cen * lax.rsqrt(var + RWKV_GN_EPS) * vec_ref[9:10, :] + vec_ref[10:11, :]
        bonus = _dot_x01(r * k2 * r_k, bones, N_STAT)
        yy_s[s, :, C_X:] = (yd + bonus * v) * g_out
        o_ref[s] = x + jnp.dot(yy_s[s].astype(BF16), wout_ref[...], preferred_element_type=F32)

    _run_skewed([stages(s) for s in range(nseq)])


def _odd_layer(h, nw, w_in, conv_w, conv_b, dt_bias, a_log, d_skip, ssd_norm_w, mu, w0, w2, a0, a2, g2,
               k_k, k_a, r_k, gn_w, gn_b, w_out):
    w_z = w_in[:, 0:512]
    w_xbc = w_in[:, 512:1536]
    w_dt = w_in[:, 1536:1544]
    w_pd = w_in[:, 1544:1544 + PD_W]
    w_pack = jnp.concatenate([w_z, w_xbc, w_pd, _pad_cols(w_dt, LANES)], axis=1).astype(BF16)
    vec = jnp.zeros((16, D_HK), F32)
    vec = vec.at[0, :SSD_HEADS].set(dt_bias).at[1, :SSD_HEADS].set(a_log)
    vec = vec.at[2].set(w0).at[3].set(a0).at[4].set(k_k).at[5].set(k_a).at[6].set(r_k.reshape(-1))
    vec = vec.at[7].set(jnp.repeat(d_skip, SSD_P)).at[8].set(ssd_norm_w).at[9].set(gn_w).at[10].set(gn_b)
    rank = w2.shape[0]
    w2p = jnp.zeros((LANES, D_HK), F32).at[:rank].set(w2).astype(BF16)
    a2p = jnp.zeros((LANES, D_HK), F32).at[rank:rank + a2.shape[0]].set(a2).astype(BF16)
    head_of_lane = jnp.arange(D_HK) // RWKV_DK
    bones = (head_of_lane[:, None] == head_of_lane[None, :]).astype(BF16)
    consts = (nw.reshape(1, D_MODEL), w_pack, conv_w, conv_b.reshape(1, -1), mu.reshape(1, PD_W),
              vec, w2p, a2p, g2.astype(BF16), _expander(SSD_HEADS, SSD_P), bones, w_out.astype(BF16))
    scratch = lambda rows: [(CARRY_ROWS, C_X + 2 * C_BC), (CARRY_ROWS, PD_W),
                            (SSD_GROUPS, SSD_STATE, GROUP_W), (RWKV_HEADS // GROUP, GROUP_W, GROUP_W),
                            (rows, D_MODEL)]
    return _mixer_call(_odd_kernel, h, consts, scratch, "odd_mixer")


def kernel(x, norm_mix_w, norm_mlp_w, mlp_w1, mlp_w2, final_norm_w, even_w_in, gdn_conv_w, gdn_a_log, gdn_dt_bias, gdn_norm_w, gla_gate_w2, gla_gate_b, gla_norm_w, even_w_out, odd_w_in, ssd_conv_w, ssd_conv_b, ssd_dt_bias, ssd_a_log, ssd_d, ssd_norm_w, rwkv_mu, rwkv_w0, rwkv_w2, rwkv_a0, rwkv_a2, rwkv_g2, rwkv_k_k, rwkv_k_a, rwkv_r_k, rwkv_gn_w, rwkv_gn_b, odd_w_out):
    bsz, seq, _ = x.shape
    h = x
    for layer in range(DEPTH):
        i = layer // 2
        if layer % 2 == 0:
            h = _even_layer(h, norm_mix_w[layer], even_w_in[i], gdn_conv_w[i], gdn_a_log[i], gdn_dt_bias[i],
                            gdn_norm_w[i], gla_gate_w2[i], gla_gate_b[i], gla_norm_w[i], even_w_out[i])
        else:
            h = _odd_layer(h, norm_mix_w[layer], odd_w_in[i], ssd_conv_w[i], ssd_conv_b[i], ssd_dt_bias[i],
                           ssd_a_log[i], ssd_d[i], ssd_norm_w[i], rwkv_mu[i], rwkv_w0[i], rwkv_w2[i],
                           rwkv_a0[i], rwkv_a2[i], rwkv_g2[i], rwkv_k_k[i], rwkv_k_a[i], rwkv_r_k[i],
                           rwkv_gn_w[i], rwkv_gn_b[i], odd_w_out[i])
        last = layer == DEPTH - 1
        h = _mlp_layer(h.reshape(bsz * seq, D_MODEL), norm_mlp_w[layer], mlp_w1[layer], mlp_w2[layer],
                       final_norm_w, last).reshape(bsz, seq, D_MODEL)
    return h
```

```python
import functools

import jax
import jax.numpy as jnp
from jax import lax
from jax.experimental import pallas as pl
from jax.experimental.pallas import tpu as pltpu

F32 = jnp.float32
BF16 = jnp.bfloat16

D_MODEL = 1024
DEPTH = 4
CHUNK = 64
CONV_W = 4
D_FF = 4 * D_MODEL
NORM_EPS = 1e-6
L2_EPS = 1e-6
MIX_W = D_MODEL // 2
GDN_HEADS = 4
GDN_DK = 128
GLA_HEADS = 4
GLA_DK = 64
GLA_DV = 128
GLA_GATE_RANK = 16
GLA_TAU = 16.0
GLA_LOG_GATE_MIN = -1.0
SSD_HEADS = 8
SSD_P = 64
SSD_GROUPS = 2
SSD_STATE = 128
RWKV_HEADS = 8
RWKV_DK = 64
RWKV_GN_EPS = 64e-5

LANES = 128
CARRY_ROWS = 8
GROUP = 4
GROUP_W = GROUP * CHUNK
SEQ_BLOCK = 512
N_EXP = 2
N_STAT = 1
MLP_ROWS = 1024
MLP_FF_BLOCK = 1024
VMEM_LIMIT = 56 * 1024 * 1024


def _mm(a, b):
    return jnp.dot(a.astype(BF16), b.astype(BF16), preferred_element_type=F32)


def _mm_nt(a, b):
    return lax.dot_general(a.astype(BF16), b.astype(BF16), (((1,), (1,)), ((), ())),
                           preferred_element_type=F32)


def _mm_tn(a, b):
    return lax.dot_general(a.astype(BF16), b.astype(BF16), (((0,), (0,)), ((), ())),
                           preferred_element_type=F32)


def _split(x, n):
    pieces = []
    rem = x
    for i in range(n):
        p = rem.astype(BF16)
        pieces.append(p)
        if i + 1 < n:
            rem = rem - p.astype(F32)
    return pieces


def _dot_x01(x, m01, n):
    return sum(jnp.dot(p, m01, preferred_element_type=F32) for p in _split(x, n))


def _dot_01x(m01, x, n):
    return sum(jnp.dot(m01, p, preferred_element_type=F32) for p in _split(x, n))


def _rms(x, w, eps=NORM_EPS):
    return x * lax.rsqrt(jnp.mean(x * x, axis=-1, keepdims=True) + eps) * w


def _softplus(x):
    return jnp.maximum(x, 0.0) + jnp.log1p(jnp.exp(-jnp.abs(x)))


def _sigmoid(x):
    return 0.5 * jnp.tanh(0.5 * x) + 0.5


def _silu(x):
    return x * _sigmoid(x)


def _iota2(shape, dim):
    return lax.broadcasted_iota(jnp.int32, shape, dim)


def _log2(n):
    assert n & (n - 1) == 0
    return n.bit_length() - 1


def _lane_blocks(nblk, width):
    lane = _iota2((1, nblk * width), 1)
    return [(lane >= h * width) & (lane < (h + 1) * width) for h in range(nblk)]


def _bd(x, masks):
    return jnp.concatenate([jnp.where(m, x, 0.0).astype(BF16) for m in masks], axis=0)


def _group_consts():
    ii = _iota2((CHUNK, GROUP_W), 0)
    jj = _iota2((CHUNK, GROUP_W), 1) & (CHUNK - 1)
    causal = ii >= jj
    strict = ii > jj
    eye = jnp.where(ii == jj, 1.0, 0.0).astype(F32)
    levels = []
    k = 0
    while (1 << k) < CHUNK:
        same = (ii >> (k + 1)) == (jj >> (k + 1))
        lower = ((ii >> k) & 1) == 1
        left = ((jj >> k) & 1) == 0
        levels.append(jnp.where(same & lower & left, 1.0, 0.0).astype(F32))
        k += 1
    ones = jnp.ones((CHUNK, CHUNK), BF16)
    return causal, strict, eye, levels, ones


def _chunk_tril(rows):
    n = min(rows, GROUP_W)
    ii = _iota2((n, n), 0)
    jj = _iota2((n, n), 1)
    return jnp.where(((ii >> _log2(CHUNK)) == (jj >> _log2(CHUNK))) & (ii >= jj), 1.0, 0.0).astype(BF16)


def _chunk_cumsum(tril, x, n):
    t = tril.shape[0]
    return jnp.concatenate([_dot_01x(tril, x[i:i + t], n) for i in range(0, x.shape[0], t)], axis=0)


def _tri_inv_many(ls, eye, levels, blocks):
    xs = [eye - l * levels[0] for l in ls]
    for lvl in levels[1:]:
        ys = [_mm(x, _bd(l * lvl, blocks)) for x, l in zip(xs, ls)]
        xs = [x - _mm(y, _bd(x, blocks)) for x, y in zip(xs, ys)]
    return xs


def _rowcast(colx, eye, ones):
    return _dot_01x(ones, colx * eye, N_EXP)


def _decay_mask(colx, mask, eye, ones):
    return jnp.where(mask, jnp.exp(jnp.where(mask, colx - _rowcast(colx, eye, ones), 0.0)), 0.0)


def _shifted_rows(cur, carry, shift):
    full = jnp.concatenate([carry, cur], axis=0)
    if shift:
        full = pltpu.roll(full, shift, 0)
    return full[CARRY_ROWS:]


def _causal_conv(cur, carry, w):
    out = None
    for j in range(CONV_W):
        term = _shifted_rows(cur, carry, CONV_W - 1 - j) * w[j:j + 1, :]
        out = term if out is None else out + term
    return out


def _mlp_kernel(x_ref, nw_ref, w1_ref, w2_ref, fw_ref, o_ref, *, final_norm):
    x = x_ref[...]
    hn = _rms(x, nw_ref[...]).astype(BF16)
    acc = x
    for c in range(D_FF // MLP_FF_BLOCK):
        cols = slice(c * MLP_FF_BLOCK, (c + 1) * MLP_FF_BLOCK)
        h = jnp.dot(hn, w1_ref[:, cols], preferred_element_type=F32)
        h = jnp.square(jnp.maximum(h, 0.0)).astype(BF16)
        acc = acc + jnp.dot(h, w2_ref[cols, :], preferred_element_type=F32)
    if final_norm:
        acc = _rms(acc, fw_ref[...])
    o_ref[...] = acc


def _mlp_layer(h2d, nw, w1, w2, fw, final_norm):
    t = h2d.shape[0]
    tm = min(MLP_ROWS, t)
    const = lambda i: (0, 0)
    return pl.pallas_call(
        functools.partial(_mlp_kernel, final_norm=final_norm),
        grid=(t // tm,),
        in_specs=[
            pl.BlockSpec((tm, D_MODEL), lambda i: (i, 0)),
            pl.BlockSpec((1, D_MODEL), const),
            pl.BlockSpec((D_MODEL, D_FF), const),
            pl.BlockSpec((D_FF, D_MODEL), const),
            pl.BlockSpec((1, D_MODEL), const),
        ],
        out_specs=pl.BlockSpec((tm, D_MODEL), lambda i: (i, 0)),
        out_shape=jax.ShapeDtypeStruct((t, D_MODEL), F32),
        compiler_params=pltpu.CompilerParams(
            dimension_semantics=("parallel",), vmem_limit_bytes=VMEM_LIMIT),
        name="mlp_final" if final_norm else "mlp",
    )(h2d, nw.reshape(1, D_MODEL), w1.astype(BF16), w2.astype(BF16), fw.reshape(1, D_MODEL))


E_A, E_B, E_GK, E_GQ, E_QKV, E_GV, E_GR, E_Z, E_END = 0, 128, 256, 512, 768, 2304, 2816, 3328, 3840
E_GLR_LANE = 8


def _even_kernel(x_ref, nw_ref, w_ref, cw_ref, vec_ref, w2p_ref, e64_ref, e128_ref, wout_ref, o_ref,
                 carry_s, sgdn_s, sgla_s, oo_s, *, rows):
    nc = rows // CHUNK

    @pl.when(pl.program_id(1) == 0)
    def _():
        carry_s[...] = jnp.zeros_like(carry_s)
        sgdn_s[...] = jnp.zeros_like(sgdn_s)
        sgla_s[...] = jnp.zeros_like(sgla_s)

    causal, strict, eye, levels, ones = _group_consts()
    tril = _chunk_tril(rows)
    blk64 = _lane_blocks(GROUP, CHUNK)
    blk128 = _lane_blocks(GDN_HEADS, GDN_DK)
    e64, e128 = e64_ref[...], e128_ref[...]
    pair_r = _iota2((2 * GDN_DK, 2 * GDN_DK), 0) >= GDN_DK
    pair_c = _iota2((2 * GDN_DK, 2 * GDN_DK), 1) >= GDN_DK
    gdn_bd = pair_r == pair_c
    gla_bd = (_iota2((GLA_HEADS * GLA_DV, GLA_HEADS * GLA_DK), 0) >> _log2(GLA_DV)) == \
             (_iota2((GLA_HEADS * GLA_DV, GLA_HEADS * GLA_DK), 1) >> _log2(GLA_DK))
    chunks = [slice(c * CHUNK, (c + 1) * CHUNK) for c in range(nc)]
    lasts = [slice((c + 1) * CHUNK - 1, (c + 1) * CHUNK) for c in range(nc)]

    x = x_ref[0]
    hn = _rms(x, nw_ref[...]).astype(BF16)
    proj = jnp.dot(hn, w_ref[...], preferred_element_type=F32)
    qkv_raw = proj[:, E_QKV:E_QKV + 3 * MIX_W]
    qkv = _silu(_causal_conv(qkv_raw, carry_s[...], cw_ref[...]))
    carry_s[...] = qkv_raw[rows - CARRY_ROWS:, :]
    qs, ks = [], []
    for h in range(GDN_HEADS):
        qh = qkv[:, h * GDN_DK:(h + 1) * GDN_DK]
        kh = qkv[:, MIX_W + h * GDN_DK:MIX_W + (h + 1) * GDN_DK]
        qs.append(qh * (lax.rsqrt(jnp.sum(qh * qh, axis=-1, keepdims=True) + L2_EPS) * GDN_DK ** -0.5))
        ks.append(kh * lax.rsqrt(jnp.sum(kh * kh, axis=-1, keepdims=True) + L2_EPS))
    q4 = jnp.concatenate(qs, axis=1)
    k4 = jnp.concatenate(ks, axis=1)
    v4 = qkv[:, 2 * MIX_W:]
    alog = vec_ref[0:1, 0:LANES]
    dtb = vec_ref[1:2, 0:LANES]
    g128 = -jnp.exp(alog) * _softplus(proj[:, E_A:E_A + LANES] + dtb)
    pb = proj[:, E_B:E_B + LANES]
    gc128 = _chunk_cumsum(tril, g128, N_EXP)
    gce64 = _dot_x01(gc128, e64, N_EXP)
    gce128 = _dot_x01(gc128, e128, N_EXP)
    beta = _dot_x01(_sigmoid(pb), e128, N_STAT)
    egc = jnp.exp(gce128)
    kb4 = k4 * beta
    kbg4 = kb4 * egc
    vb4 = v4 * beta
    qg4 = q4 * egc
    gq = proj[:, E_GQ:E_GQ + GLA_HEADS * GLA_DK] * GLA_DK ** -0.5
    gk = proj[:, E_GK:E_GK + GLA_HEADS * GLA_DK]
    gv = proj[:, E_GV:E_GV + MIX_W]
    gate = _mm(pb, w2p_ref[...]) + vec_ref[3:4, :]
    la = jnp.maximum(-_softplus(-gate) / GLA_TAU, GLA_LOG_GATE_MIN)
    lc = _chunk_cumsum(tril, la, N_EXP)
    gqg = gq * jnp.exp(lc)
    gkg = gk * jnp.exp(-lc)

    dmask = [_decay_mask(gce64[cs], causal, eye, ones) for cs in chunks]
    la_ = [_mm_nt(jnp.concatenate([kb4[cs], q4[cs]], axis=0), _bd(k4[cs], blk128)) for cs in chunks]
    a4 = [m[CHUNK:] * d for m, d in zip(la_, dmask)]
    t4 = _tri_inv_many([jnp.where(strict, m[:CHUNK] * d, 0.0) for m, d in zip(la_, dmask)],
                       eye, levels, blk64)
    wu = [_mm(t, jnp.concatenate([_bd(kbg4[cs], blk128), _bd(vb4[cs], blk128)], axis=1))
          for t, cs in zip(t4, chunks)]
    att = [jnp.where(causal, _mm_nt(gqg[cs], _bd(gkg[cs], blk64)), 0.0) for cs in chunks]
    o_intra = [_mm(a, _bd(gv[cs], blk128)) for a, cs in zip(att, chunks)]
    gla_in = [_mm_tn(gv[cs], gkg[cs] * jnp.exp(lc[last])) for cs, last in zip(chunks, lasts)]

    gdn_state = [sgdn_s[p] for p in range(GDN_HEADS // 2)]
    gla_state = sgla_s[...]
    for c, (cs, last) in enumerate(zip(chunks, lasts)):
        w4, u4 = wu[c][:, :MIX_W], wu[c][:, MIX_W:]
        qg_c = qg4[cs]
        gcl = gce128[last]
        kd4 = k4[cs] * jnp.exp(gcl - gce128[cs])
        gl = jnp.exp(gcl)
        v_new, q_s = [], []
        for p in range(GDN_HEADS // 2):
            psl = slice(p * 2 * GDN_DK, (p + 1) * 2 * GDN_DK)
            state = gdn_state[p]
            res = _mm(jnp.concatenate([w4[:, psl], qg_c[:, psl]], axis=0), state)
            v_new.append(u4[:, psl] - res[:CHUNK])
            q_s.append(res[CHUNK:])
            gdn_state[p] = jnp.where(gdn_bd, state * gl[:, psl] + _mm_tn(kd4[:, psl], v_new[p]), 0.0)
        oo_s[cs, :MIX_W] = (jnp.concatenate(q_s, axis=1)
                            + _mm(a4[c], _bd(jnp.concatenate(v_new, axis=1), blk128)))
        oo_s[cs, MIX_W:] = o_intra[c] + _mm_nt(gqg[cs], gla_state)
        gla_state = jnp.where(gla_bd, gla_state * jnp.exp(lc[last]) + gla_in[c], 0.0)
    for p in range(GDN_HEADS // 2):
        sgdn_s[p] = gdn_state[p]
    sgla_s[...] = gla_state

    gdn_w = vec_ref[2:3, 0:LANES]
    gla_w = vec_ref[4:5, 0:LANES]
    for h in range(GDN_HEADS + GLA_HEADS):
        sl = slice(h * LANES, (h + 1) * LANES)
        if h < GDN_HEADS:
            gt, nw = proj[:, E_Z + h * LANES:E_Z + (h + 1) * LANES], gdn_w
        else:
            hh = h - GDN_HEADS
            gt, nw = proj[:, E_GR + hh * LANES:E_GR + (hh + 1) * LANES], gla_w
        oo_s[:, sl] = _rms(oo_s[:, sl], nw) * _silu(gt)
    o_ref[0] = x + jnp.dot(oo_s[...].astype(BF16), wout_ref[...], preferred_element_type=F32)


def _pad_cols(w, width, at=0):
    out = jnp.zeros((w.shape[0], width), w.dtype)
    return out.at[:, at:at + w.shape[1]].set(w)


def _expander(nblk, width):
    return (jnp.arange(LANES)[:, None] == (jnp.arange(nblk * width) // width)[None, :]).astype(BF16)


def _mixer_call(body, h, consts, scratch, name):
    bsz, seq, _ = h.shape
    rows = min(SEQ_BLOCK, seq)

    def full(a):
        return pl.BlockSpec(a.shape, lambda b, j: (0,) * a.ndim)

    blk = pl.BlockSpec((1, rows, D_MODEL), lambda b, j: (b, j, 0))
    return pl.pallas_call(
        functools.partial(body, rows=rows),
        grid=(bsz, seq // rows),
        in_specs=[blk] + [full(a) for a in consts],
        out_specs=blk,
        out_shape=jax.ShapeDtypeStruct(h.shape, F32),
        scratch_shapes=[pltpu.VMEM(s, F32) for s in scratch(rows)],
        compiler_params=pltpu.CompilerParams(
            dimension_semantics=("parallel", "arbitrary"), vmem_limit_bytes=VMEM_LIMIT),
        name=name,
    )(h, *consts)


def _even_layer(h, nw, w_in, conv_w, a_log, dt_bias, gdn_norm_w, gla_w2, gla_b, gla_norm_w, w_out):
    o = 0
    parts = []
    for s in (1536, 512, 4, 4, 256, 256, 512, 512, 16):
        parts.append(w_in[:, o:o + s])
        o += s
    w_qkv, w_z, w_b, w_a, w_gq, w_gk, w_gv, w_gr, w_glr = parts
    w_b128 = _pad_cols(w_b, LANES).at[:, E_GLR_LANE:E_GLR_LANE + GLA_GATE_RANK].set(w_glr)
    w_pack = jnp.concatenate([_pad_cols(w_a, LANES), w_b128, w_gk, w_gq, w_qkv, w_gv, w_gr, w_z],
                             axis=1).astype(BF16)
    vec = jnp.zeros((8, 2 * LANES), F32)
    vec = vec.at[0, :GDN_HEADS].set(a_log).at[1, :GDN_HEADS].set(dt_bias)
    vec = vec.at[2, :LANES].set(gdn_norm_w).at[3, :].set(gla_b).at[4, :LANES].set(gla_norm_w)
    w2p = jnp.zeros((LANES, GLA_HEADS * GLA_DK), F32).at[E_GLR_LANE:E_GLR_LANE + GLA_GATE_RANK].set(gla_w2)
    consts = (nw.reshape(1, D_MODEL), w_pack, conv_w, vec, w2p.astype(BF16),
              _expander(GROUP, CHUNK), _expander(GDN_HEADS, GDN_DK), w_out.astype(BF16))
    scratch = lambda rows: [(CARRY_ROWS, 3 * MIX_W), (GDN_HEADS // 2, 2 * GDN_DK, 2 * GDN_DK),
                            (GLA_HEADS * GLA_DV, GLA_HEADS * GLA_DK), (rows, D_MODEL)]
    return _mixer_call(_even_kernel, h, consts, scratch, "even_mixer")


O_Z, O_XBC, O_PD, O_DT, O_END = 0, 512, 1536, 3328, 3456
PD_W = 1792
C_X = SSD_HEADS * SSD_P
C_BC = SSD_GROUPS * SSD_STATE
D_HK = RWKV_HEADS * RWKV_DK


def _odd_kernel(x_ref, nw_ref, w_ref, cw_ref, cb_ref, mu_ref, vec_ref,
                w2p_ref, a2p_ref, g2_ref, exp_ref, bones_ref, wout_ref, o_ref,
                carry_s, pdc_s, sssd_s, srw_s, yy_s, *, rows):
    nc = rows // CHUNK

    @pl.when(pl.program_id(1) == 0)
    def _():
        carry_s[...] = jnp.zeros_like(carry_s)
        pdc_s[...] = jnp.zeros_like(pdc_s)
        sssd_s[...] = jnp.zeros_like(sssd_s)
        srw_s[...] = jnp.zeros_like(srw_s)

    causal, strict, eye, levels, ones = _group_consts()
    tril = _chunk_tril(rows)
    blk64 = _lane_blocks(GROUP, CHUNK)
    expand = exp_ref[...]
    bones = bones_ref[...]
    rw_bd = (_iota2((GROUP_W, GROUP_W), 0) >> _log2(RWKV_DK)) == (_iota2((GROUP_W, GROUP_W), 1) >> _log2(RWKV_DK))
    chunks = [slice(c * CHUNK, (c + 1) * CHUNK) for c in range(nc)]
    lasts = [slice((c + 1) * CHUNK - 1, (c + 1) * CHUNK) for c in range(nc)]
    groups = [slice(g * GROUP_W, (g + 1) * GROUP_W) for g in range(RWKV_HEADS // GROUP)]
    units = [(cs, gsl) for cs in chunks for gsl in groups]

    x = x_ref[0]
    hn = _rms(x, nw_ref[...]).astype(BF16)
    proj = jnp.dot(hn, w_ref[...], preferred_element_type=F32)
    z = proj[:, O_Z:O_Z + C_X]
    xbc_raw = proj[:, O_XBC:O_XBC + C_X + 2 * C_BC]
    xbc = _silu(_causal_conv(xbc_raw, carry_s[...], cw_ref[...]) + cb_ref[...])
    carry_s[...] = xbc_raw[rows - CARRY_ROWS:, :]
    xs = xbc[:, :C_X]
    b_in = xbc[:, C_X:C_X + C_BC]
    c_in = xbc[:, C_X + C_BC:]
    dtb = vec_ref[0:1, 0:LANES]
    alog = vec_ref[1:2, 0:LANES]
    dt128 = _softplus(proj[:, O_DT:O_DT + LANES] + dtb)
    ac128 = _chunk_cumsum(tril, dt128 * -jnp.exp(alog), N_EXP)
    ace = _dot_x01(ac128, expand, N_EXP)
    xdt = xs * _dot_x01(dt128, expand, N_STAT)
    e_ac = jnp.exp(ace)
    pd = proj[:, O_PD:O_PD + PD_W]
    pd = pd + (_shifted_rows(pd, pdc_s[...], 1) - pd) * mu_ref[...]
    pdc_s[...] = proj[rows - CARRY_ROWS:, O_PD:O_PD + PD_W]
    r = pd[:, 0:D_HK]
    k = pd[:, D_HK:2 * D_HK]
    v = pd[:, 2 * D_HK:3 * D_HK]
    xwa = pd[:, 3 * D_HK:3 * D_HK + LANES]
    xg = pd[:, 3 * D_HK + LANES:]
    w0, a0 = vec_ref[2:3, :], vec_ref[3:4, :]
    k_k, k_a, r_k = vec_ref[4:5, :], vec_ref[5:6, :], vec_ref[6:7, :]
    w_log = -_softplus(-(w0 + _mm(jnp.tanh(xwa), w2p_ref[...]))) - 0.5
    lw = -jnp.exp(w_log)
    a = _sigmoid(a0 + _mm(xwa, a2p_ref[...]))
    g_out = _mm(_sigmoid(xg), g2_ref[...])
    kkr = k * k_k
    kk = kkr * lax.rsqrt(_dot_x01(kkr * kkr, bones, N_STAT) + L2_EPS)
    k2 = k * (1.0 + (a - 1.0) * k_a)
    be = kk * a
    pc = _chunk_cumsum(tril, lw, N_EXP)
    e_np = jnp.exp(-pc)
    a_hat = -kk * jnp.exp(pc - lw)
    b_hat = be * e_np
    k_hat = k2 * e_np
    r_hat = r * jnp.exp(pc)

    cbseg = []
    for cs in chunks:
        for g, gsl in enumerate(groups):
            ssl = slice(g * SSD_STATE, (g + 1) * SSD_STATE)
            cb4 = _mm_nt(c_in[cs, ssl], jnp.concatenate([b_in[cs, ssl].astype(BF16)] * GROUP, axis=0))
            cbseg.append(cb4 * _decay_mask(ace[cs, gsl], causal, eye, ones))
    y_intra = [_mm(m, _bd(xdt[cs, gsl], blk64)) for m, (cs, gsl) in zip(cbseg, units)]
    ar = [jnp.concatenate([a_hat[cs, gsl], r_hat[cs, gsl]], axis=0) for cs, gsl in units]
    mb = [_mm_nt(m, _bd(b_hat[cs, gsl], blk64)) for m, (cs, gsl) in zip(ar, units)]
    mk = [_mm_nt(m, _bd(k_hat[cs, gsl], blk64)) for m, (cs, gsl) in zip(ar, units)]
    m_rb = [jnp.where(causal, m[CHUNK:], 0.0) for m in mb]
    t4 = _tri_inv_many([jnp.where(strict, -m[:CHUNK], 0.0) for m in mb], eye, levels, blk64)
    mv = [_mm(jnp.concatenate([jnp.where(strict, m[:CHUNK], 0.0), jnp.where(causal, m[CHUNK:], 0.0)], axis=0),
              _bd(v[cs, gsl], blk64)) for m, (cs, gsl) in zip(mk, units)]
    tu = [_mm(t, jnp.concatenate([_bd(a_hat[cs, gsl], blk64), _bd(m[:CHUNK], blk64)], axis=1))
          for t, m, (cs, gsl) in zip(t4, mv, units)]
    p_c = [jnp.exp(pc[last]) for last in lasts]
    cd = [jnp.exp(ace[last]) for last in lasts]
    rw_mat, rw_add, ssd_add = [], [], []
    for c, (cs, last) in enumerate(zip(chunks, lasts)):
        e_dec = jnp.exp(ace[last] - ace[cs])
        for g, gsl in enumerate(groups):
            i = c * len(groups) + g
            ssl = slice(g * SSD_STATE, (g + 1) * SSD_STATE)
            b_til = b_hat[cs, gsl] * p_c[c][:, gsl]
            k_til = k_hat[cs, gsl] * p_c[c][:, gsl]
            rw_mat.append(_mm_tn(tu[i][:, :GROUP_W], b_til))
            rw_add.append(_mm_tn(jnp.concatenate([tu[i][:, GROUP_W:], v[cs, gsl]], axis=0),
                                 jnp.concatenate([b_til, k_til], axis=0)))
            ssd_add.append(_mm_tn(b_in[cs, ssl], xdt[cs, gsl] * e_dec[:, gsl]))

    rw_state = [srw_s[g] for g in range(len(groups))]
    ssd_state = [sssd_s[g] for g in range(len(groups))]
    for c, cs in enumerate(chunks):
        start = [(rw_state[g], ssd_state[g]) for g in range(len(groups))]
        for g, gsl in enumerate(groups):
            i = c * len(groups) + g
            st = rw_state[g]
            rw_state[g] = jnp.where(rw_bd, st * p_c[c][:, gsl] + _mm(st, rw_mat[i]) + rw_add[i], 0.0)
            ssd_state[g] = ssd_state[g] * cd[c][:, gsl] + ssd_add[i]
        for g, gsl in enumerate(groups):
            i = c * len(groups) + g
            st_rw, st_ssd = start[g]
            ssl = slice(g * SSD_STATE, (g + 1) * SSD_STATE)
            yy_s[cs, gsl] = y_intra[i] + _mm(c_in[cs, ssl], st_ssd) * e_ac[cs, gsl]
            res = _mm_nt(jnp.concatenate([tu[i][:, :GROUP_W], r_hat[cs, gsl]], axis=0), st_rw)
            u = tu[i][:, GROUP_W:] + res[:CHUNK]
            yy_s[cs, C_X + g * GROUP_W:C_X + (g + 1) * GROUP_W] = (
                mv[i][CHUNK:] + res[CHUNK:] + _mm(m_rb[i], _bd(u, blk64)))
    for g in range(len(groups)):
        srw_s[g] = rw_state[g]
        sssd_s[g] = ssd_state[g]

    d_skip = vec_ref[7:8, :]
    yc = (yy_s[:, 0:C_X] + xs * d_skip) * _silu(z)
    gw = C_X // SSD_GROUPS
    for g in range(SSD_GROUPS):
        gsl = slice(g * gw, (g + 1) * gw)
        yy_s[:, gsl] = _rms(yc[:, gsl], vec_ref[8:9, gsl])
    yd = yy_s[:, C_X:]
    inv_n = 1.0 / RWKV_DK
    mean = _dot_x01(yd, bones, N_STAT) * inv_n
    cen = yd - mean
    var = _dot_x01(cen * cen, bones, N_STAT) * inv_n
    yd = cen * lax.rsqrt(var + RWKV_GN_EPS) * vec_ref[9:10, :] + vec_ref[10:11, :]
    bonus = _dot_x01(r * k2 * r_k, bones, N_STAT)
    yy_s[:, C_X:] = (yd + bonus * v) * g_out
    o_ref[0] = x + jnp.dot(yy_s[...].astype(BF16), wout_ref[...], preferred_element_type=F32)


def _odd_layer(h, nw, w_in, conv_w, conv_b, dt_bias, a_log, d_skip, ssd_norm_w, mu, w0, w2, a0, a2, g2,
               k_k, k_a, r_k, gn_w, gn_b, w_out):
    w_z = w_in[:, 0:512]
    w_xbc = w_in[:, 512:1536]
    w_dt = w_in[:, 1536:1544]
    w_pd = w_in[:, 1544:1544 + PD_W]
    w_pack = jnp.concatenate([w_z, w_xbc, w_pd, _pad_cols(w_dt, LANES)], axis=1).astype(BF16)
    vec = jnp.zeros((16, D_HK), F32)
    vec = vec.at[0, :SSD_HEADS].set(dt_bias).at[1, :SSD_HEADS].set(a_log)
    vec = vec.at[2].set(w0).at[3].set(a0).at[4].set(k_k).at[5].set(k_a).at[6].set(r_k.reshape(-1))
    vec = vec.at[7].set(jnp.repeat(d_skip, SSD_P)).at[8].set(ssd_norm_w).at[9].set(gn_w).at[10].set(gn_b)
    rank = w2.shape[0]
    w2p = jnp.zeros((LANES, D_HK), F32).at[:rank].set(w2).astype(BF16)
    a2p = jnp.zeros((LANES, D_HK), F32).at[rank:rank + a2.shape[0]].set(a2).astype(BF16)
    head_of_lane = jnp.arange(D_HK) // RWKV_DK
    bones = (head_of_lane[:, None] == head_of_lane[None, :]).astype(BF16)
    consts = (nw.reshape(1, D_MODEL), w_pack, conv_w, conv_b.reshape(1, -1), mu.reshape(1, PD_W),
              vec, w2p, a2p, g2.astype(BF16), _expander(SSD_HEADS, SSD_P), bones, w_out.astype(BF16))
    scratch = lambda rows: [(CARRY_ROWS, C_X + 2 * C_BC), (CARRY_ROWS, PD_W),
                            (SSD_GROUPS, SSD_STATE, GROUP_W), (RWKV_HEADS // GROUP, GROUP_W, GROUP_W),
                            (rows, D_MODEL)]
    return _mixer_call(_odd_kernel, h, consts, scratch, "odd_mixer")


def kernel(x, norm_mix_w, norm_mlp_w, mlp_w1, mlp_w2, final_norm_w, even_w_in, gdn_conv_w, gdn_a_log, gdn_dt_bias, gdn_norm_w, gla_gate_w2, gla_gate_b, gla_norm_w, even_w_out, odd_w_in, ssd_conv_w, ssd_conv_b, ssd_dt_bias, ssd_a_log, ssd_d, ssd_norm_w, rwkv_mu, rwkv_w0, rwkv_w2, rwkv_a0, rwkv_a2, rwkv_g2, rwkv_k_k, rwkv_k_a, rwkv_r_k, rwkv_gn_w, rwkv_gn_b, odd_w_out):
    bsz, seq, _ = x.shape
    h = x
    for layer in range(DEPTH):
        i = layer // 2
        if layer % 2 == 0:
            h = _even_layer(h, norm_mix_w[layer], even_w_in[i], gdn_conv_w[i], gdn_a_log[i], gdn_dt_bias[i],
                            gdn_norm_w[i], gla_gate_w2[i], gla_gate_b[i], gla_norm_w[i], even_w_out[i])
        else:
            h = _odd_layer(h, norm_mix_w[layer], odd_w_in[i], ssd_conv_w[i], ssd_conv_b[i], ssd_dt_bias[i],
                           ssd_a_log[i], ssd_d[i], ssd_norm_w[i], rwkv_mu[i], rwkv_w0[i], rwkv_w2[i],
                           rwkv_a0[i], rwkv_a2[i], rwkv_g2[i], rwkv_k_k[i], rwkv_k_a[i], rwkv_r_k[i],
                           rwkv_gn_w[i], rwkv_gn_b[i], odd_w_out[i])
        last = layer == DEPTH - 1
        h = _mlp_layer(h.reshape(bsz * seq, D_MODEL), norm_mlp_w[layer], mlp_w1[layer], mlp_w2[layer],
                       final_norm_w, last).reshape(bsz, seq, D_MODEL)
    return h
```

```python
import functools

import jax
import jax.numpy as jnp
from jax import lax
from jax.experimental import pallas as pl
from jax.experimental.pallas import tpu as pltpu

F32 = jnp.float32
BF16 = jnp.bfloat16

D_MODEL = 1024
DEPTH = 4
CHUNK = 64
CONV_W = 4
D_FF = 4 * D_MODEL
NORM_EPS = 1e-6
L2_EPS = 1e-6
MIX_W = D_MODEL // 2
GDN_HEADS = 4
GDN_DK = 128
GLA_HEADS = 4
GLA_DK = 64
GLA_DV = 128
GLA_GATE_RANK = 16
GLA_TAU = 16.0
GLA_LOG_GATE_MIN = -1.0
SSD_HEADS = 8
SSD_P = 64
SSD_GROUPS = 2
SSD_STATE = 128
RWKV_HEADS = 8
RWKV_DK = 64
RWKV_GN_EPS = 64e-5

LANES = 128
CARRY_ROWS = 8
GROUP = 4
GROUP_W = GROUP * CHUNK
SEQ_BLOCK = 512
N_EXP = 2
N_STAT = 1
MLP_ROWS = 1024
MLP_FF_BLOCK = 1024
VMEM_LIMIT = 56 * 1024 * 1024


def _mm(a, b):
    return jnp.dot(a.astype(BF16), b.astype(BF16), preferred_element_type=F32)


def _mm_nt(a, b):
    return lax.dot_general(a.astype(BF16), b.astype(BF16), (((1,), (1,)), ((), ())),
                           preferred_element_type=F32)


def _mm_tn(a, b):
    return lax.dot_general(a.astype(BF16), b.astype(BF16), (((0,), (0,)), ((), ())),
                           preferred_element_type=F32)


def _split(x, n):
    pieces = []
    rem = x
    for i in range(n):
        p = rem.astype(BF16)
        pieces.append(p)
        if i + 1 < n:
            rem = rem - p.astype(F32)
    return pieces


def _dot_x01(x, m01, n):
    return sum(jnp.dot(p, m01, preferred_element_type=F32) for p in _split(x, n))


def _dot_01x(m01, x, n):
    return sum(jnp.dot(m01, p, preferred_element_type=F32) for p in _split(x, n))


def _rms(x, w, eps=NORM_EPS):
    return x * lax.rsqrt(jnp.mean(x * x, axis=-1, keepdims=True) + eps) * w


def _softplus(x):
    return jnp.maximum(x, 0.0) + jnp.log1p(jnp.exp(-jnp.abs(x)))


def _sigmoid(x):
    return 0.5 * jnp.tanh(0.5 * x) + 0.5


def _silu(x):
    return x * _sigmoid(x)


def _iota2(shape, dim):
    return lax.broadcasted_iota(jnp.int32, shape, dim)


def _log2(n):
    assert n & (n - 1) == 0
    return n.bit_length() - 1


def _lane_blocks(nblk, width):
    lane = _iota2((1, nblk * width), 1)
    return [(lane >= h * width) & (lane < (h + 1) * width) for h in range(nblk)]


def _bd(x, masks):
    return jnp.concatenate([jnp.where(m, x, 0.0).astype(BF16) for m in masks], axis=0)


def _group_consts():
    ii = _iota2((CHUNK, GROUP_W), 0)
    jj = _iota2((CHUNK, GROUP_W), 1) & (CHUNK - 1)
    causal = ii >= jj
    strict = ii > jj
    eye = jnp.where(ii == jj, 1.0, 0.0).astype(F32)
    levels = []
    k = 0
    while (1 << k) < CHUNK:
        same = (ii >> (k + 1)) == (jj >> (k + 1))
        lower = ((ii >> k) & 1) == 1
        left = ((jj >> k) & 1) == 0
        levels.append(jnp.where(same & lower & left, 1.0, 0.0).astype(F32))
        k += 1
    ones = jnp.ones((CHUNK, CHUNK), BF16)
    return causal, strict, eye, levels, ones


def _chunk_tril(rows):
    n = min(rows, GROUP_W)
    ii = _iota2((n, n), 0)
    jj = _iota2((n, n), 1)
    return jnp.where(((ii >> _log2(CHUNK)) == (jj >> _log2(CHUNK))) & (ii >= jj), 1.0, 0.0).astype(BF16)


def _chunk_cumsum(tril, x, n):
    t = tril.shape[0]
    return jnp.concatenate([_dot_01x(tril, x[i:i + t], n) for i in range(0, x.shape[0], t)], axis=0)


def _tri_inv_many(ls, eye, levels, blocks):
    xs = [eye - l * levels[0] for l in ls]
    for lvl in levels[1:]:
        ys = [_mm(x, _bd(l * lvl, blocks)) for x, l in zip(xs, ls)]
        xs = [x - _mm(y, _bd(x, blocks)) for x, y in zip(xs, ys)]
    return xs


def _rowcast(colx, eye, ones):
    return _dot_01x(ones, colx * eye, N_EXP)


def _decay_mask(colx, mask, eye, ones):
    return jnp.where(mask, jnp.exp(jnp.where(mask, colx - _rowcast(colx, eye, ones), 0.0)), 0.0)


def _shifted_rows(cur, carry, shift):
    full = jnp.concatenate([carry, cur], axis=0)
    if shift:
        full = pltpu.roll(full, shift, 0)
    return full[CARRY_ROWS:]


def _causal_conv(cur, carry, w):
    out = None
    for j in range(CONV_W):
        term = _shifted_rows(cur, carry, CONV_W - 1 - j) * w[j:j + 1, :]
        out = term if out is None else out + term
    return out


def _mlp_kernel(x_ref, nw_ref, w1_ref, w2_ref, fw_ref, o_ref, *, final_norm):
    x = x_ref[...]
    hn = _rms(x, nw_ref[...]).astype(BF16)
    acc = x
    for c in range(D_FF // MLP_FF_BLOCK):
        cols = slice(c * MLP_FF_BLOCK, (c + 1) * MLP_FF_BLOCK)
        h = jnp.dot(hn, w1_ref[:, cols], preferred_element_type=F32)
        h = jnp.square(jnp.maximum(h, 0.0)).astype(BF16)
        acc = acc + jnp.dot(h, w2_ref[cols, :], preferred_element_type=F32)
    if final_norm:
        acc = _rms(acc, fw_ref[...])
    o_ref[...] = acc


def _mlp_layer(h2d, nw, w1, w2, fw, final_norm):
    t = h2d.shape[0]
    tm = min(MLP_ROWS, t)
    const = lambda i: (0, 0)
    return pl.pallas_call(
        functools.partial(_mlp_kernel, final_norm=final_norm),
        grid=(t // tm,),
        in_specs=[
            pl.BlockSpec((tm, D_MODEL), lambda i: (i, 0)),
            pl.BlockSpec((1, D_MODEL), const),
            pl.BlockSpec((D_MODEL, D_FF), const),
            pl.BlockSpec((D_FF, D_MODEL), const),
            pl.BlockSpec((1, D_MODEL), const),
        ],
        out_specs=pl.BlockSpec((tm, D_MODEL), lambda i: (i, 0)),
        out_shape=jax.ShapeDtypeStruct((t, D_MODEL), F32),
        compiler_params=pltpu.CompilerParams(
            dimension_semantics=("parallel",), vmem_limit_bytes=VMEM_LIMIT),
        name="mlp_final" if final_norm else "mlp",
    )(h2d, nw.reshape(1, D_MODEL), w1.astype(BF16), w2.astype(BF16), fw.reshape(1, D_MODEL))


GLA_QK = GLA_HEADS * GLA_DK
E_WIDTHS = (LANES, LANES, GLA_QK, GLA_QK, 3 * MIX_W, MIX_W, MIX_W, MIX_W)
E_A, E_B, E_GK, E_GQ, E_QKV, E_GV, E_GR, E_Z, E_END = (sum(E_WIDTHS[:i]) for i in range(len(E_WIDTHS) + 1))
E_GLR_LANE = 8


def _even_kernel(x_ref, nw_ref, w_ref, cw_ref, vec_ref, w2p_ref, e64_ref, e128_ref, wout_ref, o_ref,
                 carry_s, sgdn_s, sgla_s, oo_s, *, rows):
    nc = rows // CHUNK

    @pl.when(pl.program_id(1) == 0)
    def _():
        carry_s[...] = jnp.zeros_like(carry_s)
        sgdn_s[...] = jnp.zeros_like(sgdn_s)
        sgla_s[...] = jnp.zeros_like(sgla_s)

    causal, strict, eye, levels, ones = _group_consts()
    tril = _chunk_tril(rows)
    blk64 = _lane_blocks(GROUP, CHUNK)
    blk128 = _lane_blocks(GDN_HEADS, GDN_DK)
    e64, e128 = e64_ref[...], e128_ref[...]
    pair_r = _iota2((2 * GDN_DK, 2 * GDN_DK), 0) >= GDN_DK
    pair_c = _iota2((2 * GDN_DK, 2 * GDN_DK), 1) >= GDN_DK
    gdn_bd = pair_r == pair_c
    gla_bd = (_iota2((GLA_HEADS * GLA_DV, GLA_HEADS * GLA_DK), 0) >> _log2(GLA_DV)) == \
             (_iota2((GLA_HEADS * GLA_DV, GLA_HEADS * GLA_DK), 1) >> _log2(GLA_DK))
    chunks = [slice(c * CHUNK, (c + 1) * CHUNK) for c in range(nc)]
    lasts = [slice((c + 1) * CHUNK - 1, (c + 1) * CHUNK) for c in range(nc)]

    x = x_ref[0]
    hn = _rms(x, nw_ref[...]).astype(BF16)
    proj = jnp.dot(hn, w_ref[...], preferred_element_type=F32)
    qkv_raw = proj[:, E_QKV:E_QKV + 3 * MIX_W]
    qkv = _silu(_causal_conv(qkv_raw, carry_s[...], cw_ref[...]))
    carry_s[...] = qkv_raw[rows - CARRY_ROWS:, :]
    qs, ks = [], []
    for h in range(GDN_HEADS):
        qh = qkv[:, h * GDN_DK:(h + 1) * GDN_DK]
        kh = qkv[:, MIX_W + h * GDN_DK:MIX_W + (h + 1) * GDN_DK]
        qs.append(qh * (lax.rsqrt(jnp.sum(qh * qh, axis=-1, keepdims=True) + L2_EPS) * GDN_DK ** -0.5))
        ks.append(kh * lax.rsqrt(jnp.sum(kh * kh, axis=-1, keepdims=True) + L2_EPS))
    q4 = jnp.concatenate(qs, axis=1)
    k4 = jnp.concatenate(ks, axis=1)
    v4 = qkv[:, 2 * MIX_W:]
    alog = vec_ref[0:1, 0:LANES]
    dtb = vec_ref[1:2, 0:LANES]
    g128 = -jnp.exp(alog) * _softplus(proj[:, E_A:E_A + LANES] + dtb)
    pb = proj[:, E_B:E_B + LANES]
    gc128 = _chunk_cumsum(tril, g128, N_EXP)
    gce64 = _dot_x01(gc128, e64, N_EXP)
    gce128 = _dot_x01(gc128, e128, N_EXP)
    beta = _dot_x01(_sigmoid(pb), e128, N_STAT)
    egc = jnp.exp(gce128)
    kb4 = k4 * beta
    kbg4 = kb4 * egc
    vb4 = v4 * beta
    qg4 = q4 * egc
    gq = proj[:, E_GQ:E_GQ + GLA_HEADS * GLA_DK] * GLA_DK ** -0.5
    gk = proj[:, E_GK:E_GK + GLA_HEADS * GLA_DK]
    gv = proj[:, E_GV:E_GV + MIX_W]
    gate = _mm(pb, w2p_ref[...]) + vec_ref[3:4, :]
    la = jnp.maximum(-_softplus(-gate) / GLA_TAU, GLA_LOG_GATE_MIN)
    lc = _chunk_cumsum(tril, la, N_EXP)
    gqg = gq * jnp.exp(lc)
    gkg = gk * jnp.exp(-lc)

    dmask = [_decay_mask(gce64[cs], causal, eye, ones) for cs in chunks]
    la_ = [_mm_nt(jnp.concatenate([kb4[cs], q4[cs]], axis=0), _bd(k4[cs], blk128)) for cs in chunks]
    a4 = [m[CHUNK:] * d for m, d in zip(la_, dmask)]
    t4 = _tri_inv_many([jnp.where(strict, m[:CHUNK] * d, 0.0) for m, d in zip(la_, dmask)],
                       eye, levels, blk64)
    wu = [_mm(t, jnp.concatenate([_bd(kbg4[cs], blk128), _bd(vb4[cs], blk128)], axis=1))
          for t, cs in zip(t4, chunks)]
    att = [jnp.where(causal, _mm_nt(gqg[cs], _bd(gkg[cs], blk64)), 0.0) for cs in chunks]
    o_intra = [_mm(a, _bd(gv[cs], blk128)) for a, cs in zip(att, chunks)]
    gla_in = [_mm_tn(gv[cs], gkg[cs] * jnp.exp(lc[last])) for cs, last in zip(chunks, lasts)]

    gdn_state = [sgdn_s[p] for p in range(GDN_HEADS // 2)]
    gla_state = sgla_s[...]
    for c, (cs, last) in enumerate(zip(chunks, lasts)):
        w4, u4 = wu[c][:, :MIX_W], wu[c][:, MIX_W:]
        qg_c = qg4[cs]
        gcl = gce128[last]
        kd4 = k4[cs] * jnp.exp(gcl - gce128[cs])
        gl = jnp.exp(gcl)
        v_new, q_s = [], []
        for p in range(GDN_HEADS // 2):
            psl = slice(p * 2 * GDN_DK, (p + 1) * 2 * GDN_DK)
            state = gdn_state[p]
            res = _mm(jnp.concatenate([w4[:, psl], qg_c[:, psl]], axis=0), state)
            v_new.append(u4[:, psl] - res[:CHUNK])
            q_s.append(res[CHUNK:])
            gdn_state[p] = jnp.where(gdn_bd, state * gl[:, psl] + _mm_tn(kd4[:, psl], v_new[p]), 0.0)
        oo_s[cs, :MIX_W] = (jnp.concatenate(q_s, axis=1)
                            + _mm(a4[c], _bd(jnp.concatenate(v_new, axis=1), blk128)))
        oo_s[cs, MIX_W:] = o_intra[c] + _mm_nt(gqg[cs], gla_state)
        gla_state = jnp.where(gla_bd, gla_state * jnp.exp(lc[last]) + gla_in[c], 0.0)
    for p in range(GDN_HEADS // 2):
        sgdn_s[p] = gdn_state[p]
    sgla_s[...] = gla_state

    gdn_w = vec_ref[2:3, 0:LANES]
    gla_w = vec_ref[4:5, 0:LANES]
    for h in range(GDN_HEADS + GLA_HEADS):
        sl = slice(h * LANES, (h + 1) * LANES)
        if h < GDN_HEADS:
            gt, nw = proj[:, E_Z + h * LANES:E_Z + (h + 1) * LANES], gdn_w
        else:
            hh = h - GDN_HEADS
            gt, nw = proj[:, E_GR + hh * LANES:E_GR + (hh + 1) * LANES], gla_w
        oo_s[:, sl] = _rms(oo_s[:, sl], nw) * _silu(gt)
    o_ref[0] = x + jnp.dot(oo_s[...].astype(BF16), wout_ref[...], preferred_element_type=F32)


def _pad_cols(w, width, at=0):
    out = jnp.zeros((w.shape[0], width), w.dtype)
    return out.at[:, at:at + w.shape[1]].set(w)


def _expander(nblk, width):
    return (jnp.arange(LANES)[:, None] == (jnp.arange(nblk * width) // width)[None, :]).astype(BF16)


def _mixer_call(body, h, consts, scratch, name):
    bsz, seq, _ = h.shape
    rows = min(SEQ_BLOCK, seq)

    def full(a):
        return pl.BlockSpec(a.shape, lambda b, j: (0,) * a.ndim)

    blk = pl.BlockSpec((1, rows, D_MODEL), lambda b, j: (b, j, 0))
    return pl.pallas_call(
        functools.partial(body, rows=rows),
        grid=(bsz, seq // rows),
        in_specs=[blk] + [full(a) for a in consts],
        out_specs=blk,
        out_shape=jax.ShapeDtypeStruct(h.shape, F32),
        scratch_shapes=[pltpu.VMEM(s, F32) for s in scratch(rows)],
        compiler_params=pltpu.CompilerParams(
            dimension_semantics=("parallel", "arbitrary"), vmem_limit_bytes=VMEM_LIMIT),
        name=name,
    )(h, *consts)


def _even_layer(h, nw, w_in, conv_w, a_log, dt_bias, gdn_norm_w, gla_w2, gla_b, gla_norm_w, w_out):
    o = 0
    parts = []
    for s in (3 * MIX_W, MIX_W, GDN_HEADS, GDN_HEADS, GLA_QK, GLA_QK, MIX_W, MIX_W, GLA_GATE_RANK):
        parts.append(w_in[:, o:o + s])
        o += s
    w_qkv, w_z, w_b, w_a, w_gq, w_gk, w_gv, w_gr, w_glr = parts
    w_b128 = _pad_cols(w_b, LANES).at[:, E_GLR_LANE:E_GLR_LANE + GLA_GATE_RANK].set(w_glr)
    w_pack = jnp.concatenate([_pad_cols(w_a, LANES), w_b128, w_gk, w_gq, w_qkv, w_gv, w_gr, w_z],
                             axis=1).astype(BF16)
    vec = jnp.zeros((8, 2 * LANES), F32)
    vec = vec.at[0, :GDN_HEADS].set(a_log).at[1, :GDN_HEADS].set(dt_bias)
    vec = vec.at[2, :LANES].set(gdn_norm_w).at[3, :].set(gla_b).at[4, :LANES].set(gla_norm_w)
    w2p = jnp.zeros((LANES, GLA_HEADS * GLA_DK), F32).at[E_GLR_LANE:E_GLR_LANE + GLA_GATE_RANK].set(gla_w2)
    consts = (nw.reshape(1, D_MODEL), w_pack, conv_w, vec, w2p.astype(BF16),
              _expander(GROUP, CHUNK), _expander(GDN_HEADS, GDN_DK), w_out.astype(BF16))
    scratch = lambda rows: [(CARRY_ROWS, 3 * MIX_W), (GDN_HEADS // 2, 2 * GDN_DK, 2 * GDN_DK),
                            (GLA_HEADS * GLA_DV, GLA_HEADS * GLA_DK), (rows, D_MODEL)]
    return _mixer_call(_even_kernel, h, consts, scratch, "even_mixer")


C_X = SSD_HEADS * SSD_P
C_BC = SSD_GROUPS * SSD_STATE
D_HK = RWKV_HEADS * RWKV_DK
PD_W = 3 * D_HK + 2 * LANES
O_WIDTHS = (C_X, C_X + 2 * C_BC, PD_W, LANES)
O_Z, O_XBC, O_PD, O_DT, O_END = (sum(O_WIDTHS[:i]) for i in range(len(O_WIDTHS) + 1))


def _odd_kernel(x_ref, nw_ref, w_ref, cw_ref, cb_ref, mu_ref, vec_ref,
                w2p_ref, a2p_ref, g2_ref, exp_ref, bones_ref, wout_ref, o_ref,
                carry_s, pdc_s, sssd_s, srw_s, yy_s, *, rows):
    nc = rows // CHUNK

    @pl.when(pl.program_id(1) == 0)
    def _():
        carry_s[...] = jnp.zeros_like(carry_s)
        pdc_s[...] = jnp.zeros_like(pdc_s)
        sssd_s[...] = jnp.zeros_like(sssd_s)
        srw_s[...] = jnp.zeros_like(srw_s)

    causal, strict, eye, levels, ones = _group_consts()
    tril = _chunk_tril(rows)
    blk64 = _lane_blocks(GROUP, CHUNK)
    expand = exp_ref[...]
    bones = bones_ref[...]
    rw_bd = (_iota2((GROUP_W, GROUP_W), 0) >> _log2(RWKV_DK)) == (_iota2((GROUP_W, GROUP_W), 1) >> _log2(RWKV_DK))
    chunks = [slice(c * CHUNK, (c + 1) * CHUNK) for c in range(nc)]
    lasts = [slice((c + 1) * CHUNK - 1, (c + 1) * CHUNK) for c in range(nc)]
    groups = [slice(g * GROUP_W, (g + 1) * GROUP_W) for g in range(RWKV_HEADS // GROUP)]
    units = [(cs, gsl) for cs in chunks for gsl in groups]

    x = x_ref[0]
    hn = _rms(x, nw_ref[...]).astype(BF16)
    proj = jnp.dot(hn, w_ref[...], preferred_element_type=F32)
    z = proj[:, O_Z:O_Z + C_X]
    xbc_raw = proj[:, O_XBC:O_XBC + C_X + 2 * C_BC]
    xbc = _silu(_causal_conv(xbc_raw, carry_s[...], cw_ref[...]) + cb_ref[...])
    carry_s[...] = xbc_raw[rows - CARRY_ROWS:, :]
    xs = xbc[:, :C_X]
    b_in = xbc[:, C_X:C_X + C_BC]
    c_in = xbc[:, C_X + C_BC:]
    dtb = vec_ref[0:1, 0:LANES]
    alog = vec_ref[1:2, 0:LANES]
    dt128 = _softplus(proj[:, O_DT:O_DT + LANES] + dtb)
    ac128 = _chunk_cumsum(tril, dt128 * -jnp.exp(alog), N_EXP)
    ace = _dot_x01(ac128, expand, N_EXP)
    xdt = xs * _dot_x01(dt128, expand, N_STAT)
    e_ac = jnp.exp(ace)
    pd = proj[:, O_PD:O_PD + PD_W]
    pd = pd + (_shifted_rows(pd, pdc_s[...], 1) - pd) * mu_ref[...]
    pdc_s[...] = proj[rows - CARRY_ROWS:, O_PD:O_PD + PD_W]
    r = pd[:, 0:D_HK]
    k = pd[:, D_HK:2 * D_HK]
    v = pd[:, 2 * D_HK:3 * D_HK]
    xwa = pd[:, 3 * D_HK:3 * D_HK + LANES]
    xg = pd[:, 3 * D_HK + LANES:]
    w0, a0 = vec_ref[2:3, :], vec_ref[3:4, :]
    k_k, k_a, r_k = vec_ref[4:5, :], vec_ref[5:6, :], vec_ref[6:7, :]
    w_log = -_softplus(-(w0 + _mm(jnp.tanh(xwa), w2p_ref[...]))) - 0.5
    lw = -jnp.exp(w_log)
    a = _sigmoid(a0 + _mm(xwa, a2p_ref[...]))
    g_out = _mm(_sigmoid(xg), g2_ref[...])
    kkr = k * k_k
    kk = kkr * lax.rsqrt(_dot_x01(kkr * kkr, bones, N_STAT) + L2_EPS)
    k2 = k * (1.0 + (a - 1.0) * k_a)
    be = kk * a
    pc = _chunk_cumsum(tril, lw, N_EXP)
    e_np = jnp.exp(-pc)
    a_hat = -kk * jnp.exp(pc - lw)
    b_hat = be * e_np
    k_hat = k2 * e_np
    r_hat = r * jnp.exp(pc)

    cbseg = []
    for cs in chunks:
        for g, gsl in enumerate(groups):
            ssl = slice(g * SSD_STATE, (g + 1) * SSD_STATE)
            cb4 = _mm_nt(c_in[cs, ssl], jnp.concatenate([b_in[cs, ssl].astype(BF16)] * GROUP, axis=0))
            cbseg.append(cb4 * _decay_mask(ace[cs, gsl], causal, eye, ones))
    y_intra = [_mm(m, _bd(xdt[cs, gsl], blk64)) for m, (cs, gsl) in zip(cbseg, units)]
    ar = [jnp.concatenate([a_hat[cs, gsl], r_hat[cs, gsl]], axis=0) for cs, gsl in units]
    mb = [_mm_nt(m, _bd(b_hat[cs, gsl], blk64)) for m, (cs, gsl) in zip(ar, units)]
    mk = [_mm_nt(m, _bd(k_hat[cs, gsl], blk64)) for m, (cs, gsl) in zip(ar, units)]
    m_rb = [jnp.where(causal, m[CHUNK:], 0.0) for m in mb]
    t4 = _tri_inv_many([jnp.where(strict, -m[:CHUNK], 0.0) for m in mb], eye, levels, blk64)
    mv = [_mm(jnp.concatenate([jnp.where(strict, m[:CHUNK], 0.0), jnp.where(causal, m[CHUNK:], 0.0)], axis=0),
              _bd(v[cs, gsl], blk64)) for m, (cs, gsl) in zip(mk, units)]
    tu = [_mm(t, jnp.concatenate([_bd(a_hat[cs, gsl], blk64), _bd(m[:CHUNK], blk64)], axis=1))
          for t, m, (cs, gsl) in zip(t4, mv, units)]
    p_c = [jnp.exp(pc[last]) for last in lasts]
    cd = [jnp.exp(ace[last]) for last in lasts]
    rw_mat, rw_add, ssd_add = [], [], []
    for c, (cs, last) in enumerate(zip(chunks, lasts)):
        e_dec = jnp.exp(ace[last] - ace[cs])
        for g, gsl in enumerate(groups):
            i = c * len(groups) + g
            ssl = slice(g * SSD_STATE, (g + 1) * SSD_STATE)
            b_til = b_hat[cs, gsl] * p_c[c][:, gsl]
            k_til = k_hat[cs, gsl] * p_c[c][:, gsl]
            rw_mat.append(_mm_tn(tu[i][:, :GROUP_W], b_til))
            rw_add.append(_mm_tn(jnp.concatenate([tu[i][:, GROUP_W:], v[cs, gsl]], axis=0),
                                 jnp.concatenate([b_til, k_til], axis=0)))
            ssd_add.append(_mm_tn(b_in[cs, ssl], xdt[cs, gsl] * e_dec[:, gsl]))

    rw_state = [srw_s[g] for g in range(len(groups))]
    ssd_state = [sssd_s[g] for g in range(len(groups))]
    for c, cs in enumerate(chunks):
        start = [(rw_state[g], ssd_state[g]) for g in range(len(groups))]
        for g, gsl in enumerate(groups):
            i = c * len(groups) + g
            st = rw_state[g]
            rw_state[g] = jnp.where(rw_bd, st * p_c[c][:, gsl] + _mm(st, rw_mat[i]) + rw_add[i], 0.0)
            ssd_state[g] = ssd_state[g] * cd[c][:, gsl] + ssd_add[i]
        for g, gsl in enumerate(groups):
            i = c * len(groups) + g
            st_rw, st_ssd = start[g]
            ssl = slice(g * SSD_STATE, (g + 1) * SSD_STATE)
            yy_s[cs, gsl] = y_intra[i] + _mm(c_in[cs, ssl], st_ssd) * e_ac[cs, gsl]
            res = _mm_nt(jnp.concatenate([tu[i][:, :GROUP_W], r_hat[cs, gsl]], axis=0), st_rw)
            u = tu[i][:, GROUP_W:] + res[:CHUNK]
            yy_s[cs, C_X + g * GROUP_W:C_X + (g + 1) * GROUP_W] = (
                mv[i][CHUNK:] + res[CHUNK:] + _mm(m_rb[i], _bd(u, blk64)))
    for g in range(len(groups)):
        srw_s[g] = rw_state[g]
        sssd_s[g] = ssd_state[g]

    d_skip = vec_ref[7:8, :]
    yc = (yy_s[:, 0:C_X] + xs * d_skip) * _silu(z)
    gw = C_X // SSD_GROUPS
    for g in range(SSD_GROUPS):
        gsl = slice(g * gw, (g + 1) * gw)
        yy_s[:, gsl] = _rms(yc[:, gsl], vec_ref[8:9, gsl])
    yd = yy_s[:, C_X:]
    inv_n = 1.0 / RWKV_DK
    mean = _dot_x01(yd, bones, N_STAT) * inv_n
    cen = yd - mean
    var = _dot_x01(cen * cen, bones, N_STAT) * inv_n
    yd = cen * lax.rsqrt(var + RWKV_GN_EPS) * vec_ref[9:10, :] + vec_ref[10:11, :]
    bonus = _dot_x01(r * k2 * r_k, bones, N_STAT)
    yy_s[:, C_X:] = (yd + bonus * v) * g_out
    o_ref[0] = x + jnp.dot(yy_s[...].astype(BF16), wout_ref[...], preferred_element_type=F32)


def _odd_layer(h, nw, w_in, conv_w, conv_b, dt_bias, a_log, d_skip, ssd_norm_w, mu, w0, w2, a0, a2, g2,
               k_k, k_a, r_k, gn_w, gn_b, w_out):
    o = 0
    parts = []
    for s in (C_X, C_X + 2 * C_BC, SSD_HEADS, PD_W):
        parts.append(w_in[:, o:o + s])
        o += s
    w_z, w_xbc, w_dt, w_pd = parts
    w_pack = jnp.concatenate([w_z, w_xbc, w_pd, _pad_cols(w_dt, LANES)], axis=1).astype(BF16)
    vec = jnp.zeros((16, D_HK), F32)
    vec = vec.at[0, :SSD_HEADS].set(dt_bias).at[1, :SSD_HEADS].set(a_log)
    vec = vec.at[2].set(w0).at[3].set(a0).at[4].set(k_k).at[5].set(k_a).at[6].set(r_k.reshape(-1))
    vec = vec.at[7].set(jnp.repeat(d_skip, SSD_P)).at[8].set(ssd_norm_w).at[9].set(gn_w).at[10].set(gn_b)
    rank = w2.shape[0]
    w2p = jnp.zeros((LANES, D_HK), F32).at[:rank].set(w2).astype(BF16)
    a2p = jnp.zeros((LANES, D_HK), F32).at[rank:rank + a2.shape[0]].set(a2).astype(BF16)
    head_of_lane = jnp.arange(D_HK) // RWKV_DK
    bones = (head_of_lane[:, None] == head_of_lane[None, :]).astype(BF16)
    consts = (nw.reshape(1, D_MODEL), w_pack, conv_w, conv_b.reshape(1, -1), mu.reshape(1, PD_W),
              vec, w2p, a2p, g2.astype(BF16), _expander(SSD_HEADS, SSD_P), bones, w_out.astype(BF16))
    scratch = lambda rows: [(CARRY_ROWS, C_X + 2 * C_BC), (CARRY_ROWS, PD_W),
                            (SSD_GROUPS, SSD_STATE, GROUP_W), (RWKV_HEADS // GROUP, GROUP_W, GROUP_W),
                            (rows, D_MODEL)]
    return _mixer_call(_odd_kernel, h, consts, scratch, "odd_mixer")


def kernel(x, norm_mix_w, norm_mlp_w, mlp_w1, mlp_w2, final_norm_w, even_w_in, gdn_conv_w, gdn_a_log, gdn_dt_bias, gdn_norm_w, gla_gate_w2, gla_gate_b, gla_norm_w, even_w_out, odd_w_in, ssd_conv_w, ssd_conv_b, ssd_dt_bias, ssd_a_log, ssd_d, ssd_norm_w, rwkv_mu, rwkv_w0, rwkv_w2, rwkv_a0, rwkv_a2, rwkv_g2, rwkv_k_k, rwkv_k_a, rwkv_r_k, rwkv_gn_w, rwkv_gn_b, odd_w_out):
    bsz, seq, _ = x.shape
    h = x
    for layer in range(DEPTH):
        i = layer // 2
        if layer % 2 == 0:
            h = _even_layer(h, norm_mix_w[layer], even_w_in[i], gdn_conv_w[i], gdn_a_log[i], gdn_dt_bias[i],
                            gdn_norm_w[i], gla_gate_w2[i], gla_gate_b[i], gla_norm_w[i], even_w_out[i])
        else:
            h = _odd_layer(h, norm_mix_w[layer], odd_w_in[i], ssd_conv_w[i], ssd_conv_b[i], ssd_dt_bias[i],
                           ssd_a_log[i], ssd_d[i], ssd_norm_w[i], rwkv_mu[i], rwkv_w0[i], rwkv_w2[i],
                           rwkv_a0[i], rwkv_a2[i], rwkv_g2[i], rwkv_k_k[i], rwkv_k_a[i], rwkv_r_k[i],
                           rwkv_gn_w[i], rwkv_gn_b[i], odd_w_out[i])
        last = layer == DEPTH - 1
        h = _mlp_layer(h.reshape(bsz * seq, D_MODEL), norm_mlp_w[layer], mlp_w1[layer], mlp_w2[layer],
                       final_norm_w, last).reshape(bsz, seq, D_MODEL)
    return h
```
